```python
import math
import jax, jax.numpy as jnp
from jax import lax
import numpy as np

D_MODEL = 4096
BATCH = 1
SEQ = 8192
DEPTH = 1

MIX_WIDTH = D_MODEL
CONV_WIDTH_CH = MIX_WIDTH // 2
ATTN_HEADS = 16
HEAD_DIM = 128
ATTN_WIDTH = ATTN_HEADS * HEAD_DIM
CONV_KERNEL = 31
DILATED_PATTERNS = ((128, 1), (512, 4), (2048, 16))
ATTN_BLOCK = 128
REL_BUCKETS = 32
REL_MAX_DISTANCE = 1024
NORM_EPS = 1e-6
LN_EPS = 1e-5
NEG_INF = -1e30
IN_SPLITS = (CONV_WIDTH_CH, CONV_WIDTH_CH, CONV_WIDTH_CH,
             ATTN_WIDTH, ATTN_WIDTH, ATTN_WIDTH, ATTN_WIDTH)
IN_WIDTH = sum(IN_SPLITS)

kernel_name = "hybrid_conformer_conv_dilated_window_attn_encoder"


def rms_norm(x, g, eps=NORM_EPS):
    xf = x.astype(jnp.float32)
    y = xf * lax.rsqrt(jnp.mean(xf * xf, axis=-1, keepdims=True) + eps) * g.astype(jnp.float32)
    return y.astype(x.dtype)


def t5_relative_bucket(rel):
    half = REL_BUCKETS // 2
    exact = half // 2
    n = jnp.abs(rel)
    nf = jnp.maximum(n, 1).astype(jnp.float32)
    large = exact + (jnp.log(nf / exact) / math.log(REL_MAX_DISTANCE / exact)
                     * (half - exact)).astype(jnp.int32)
    large = jnp.minimum(large, half - 1)
    return jnp.where(rel > 0, half, 0) + jnp.where(n < exact, n, large)


def dilated_window_attention(q, k, v, rel_bias, dilation, radius):
    B, S, H, Dh = q.shape
    L = S // dilation
    N = B * dilation

    def to_sub(t):
        return t.reshape(B, L, dilation, H, Dh).transpose(0, 2, 1, 3, 4).reshape(N, L, H, Dh)

    qs, ks, vs = to_sub(q), to_sub(k), to_sub(v)
    bq = min(ATTN_BLOCK, L)
    nb = -(-L // bq)
    Lp = nb * bq
    bk = bq + 2 * radius
    qs = jnp.pad(qs, ((0, 0), (0, Lp - L), (0, 0), (0, 0))).reshape(N, nb, bq, H, Dh)
    kpad = ((0, 0), (radius, radius + Lp - L), (0, 0), (0, 0))
    key_idx = (jnp.arange(nb) * bq)[:, None] + jnp.arange(bk)[None, :]
    kb = jnp.pad(ks, kpad)[:, key_idx]
    vb = jnp.pad(vs, kpad)[:, key_idx]

    rel = jnp.arange(bk)[None, :] - radius - jnp.arange(bq)[:, None]
    bias = rel_bias[t5_relative_bucket(rel * dilation)]
    bias = bias.transpose(2, 0, 1).astype(jnp.float32)
    key_pos = key_idx - radius
    valid = (jnp.abs(rel) <= radius)[None] & ((key_pos >= 0) & (key_pos < L))[:, None, :]

    s = jnp.einsum('nbqhd,nbkhd->nbhqk', qs.astype(jnp.float32), kb.astype(jnp.float32))
    s = s * (Dh ** -0.5) + bias
    s = jnp.where(valid[None, :, None], s, NEG_INF)
    lse = jax.nn.logsumexp(s, axis=-1)
    p = jnp.exp(s - lse[..., None])
    o = jnp.einsum('nbhqk,nbkhd->nbqhd', p, vb.astype(jnp.float32))
    o = o.reshape(N, Lp, H, Dh)[:, :L]
    lse = lse.transpose(0, 1, 3, 2).reshape(N, Lp, H)[:, :L]
    o = o.reshape(B, dilation, L, H, Dh).transpose(0, 2, 1, 3, 4).reshape(B, S, H, Dh)
    lse = lse.reshape(B, dilation, L, H).transpose(0, 2, 1, 3).reshape(B, S, H)
    return o, lse


def longnet_mixture(q, k, v, rel_bias):
    outs, lses = [], []
    for window, dilation in DILATED_PATTERNS:
        o, l = dilated_window_attention(q, k, v, rel_bias, dilation, window // (2 * dilation))
        outs.append(o)
        lses.append(l)
    w = jax.nn.softmax(jnp.stack(lses, axis=0), axis=0)
    return jnp.sum(w[..., None] * jnp.stack(outs, axis=0), axis=0)


def conformer_conv(u, glu_gate, w_dw, b_dw, ln_g, ln_b):
    a = u * jax.nn.sigmoid(glu_gate)
    c = lax.conv_general_dilated(
        a, w_dw[:, None, :].astype(a.dtype), window_strides=(1,),
        padding=[(CONV_KERNEL // 2, CONV_KERNEL // 2)],
        dimension_numbers=('NWC', 'WIO', 'NWC'), feature_group_count=a.shape[-1]) + b_dw
    cf = c.astype(jnp.float32)
    mu = jnp.mean(cf, axis=-1, keepdims=True)
    var = jnp.mean(jnp.square(cf - mu), axis=-1, keepdims=True)
    cn = (cf - mu) * lax.rsqrt(var + LN_EPS) * ln_g.astype(jnp.float32) + ln_b.astype(jnp.float32)
    return jax.nn.silu(cn).astype(u.dtype)


def setup_inputs(seed: int = 0) -> dict:
    key = jax.random.key(seed)
    ks = jax.random.split(key, 12)
    f32 = jnp.float32
    x = jax.random.normal(ks[0], (BATCH, SEQ, D_MODEL), f32)
    norm_g = 1.0 + 0.02 * jax.random.normal(ks[1], (DEPTH, D_MODEL), f32)
    w_in = jax.random.normal(ks[2], (DEPTH, D_MODEL, IN_WIDTH), f32) * D_MODEL ** -0.5
    q_norm_g = 1.0 + 0.02 * jax.random.normal(ks[3], (DEPTH, HEAD_DIM), f32)
    k_norm_g = 1.0 + 0.02 * jax.random.normal(ks[4], (DEPTH, HEAD_DIM), f32)
    rel_bias = 0.5 * jax.random.normal(ks[5], (REL_BUCKETS, ATTN_HEADS), f32)
    conv_w = jax.random.normal(ks[6], (DEPTH, CONV_KERNEL, CONV_WIDTH_CH), f32) * CONV_KERNEL ** -0.5
    conv_b = 0.02 * jax.random.normal(ks[7], (DEPTH, CONV_WIDTH_CH), f32)
    conv_ln_g = 1.0 + 0.02 * jax.random.normal(ks[8], (DEPTH, CONV_WIDTH_CH), f32)
    conv_ln_b = 0.02 * jax.random.normal(ks[9], (DEPTH, CONV_WIDTH_CH), f32)
    w_out = jax.random.normal(ks[10], (DEPTH, MIX_WIDTH, D_MODEL), f32) * MIX_WIDTH ** -0.5
    return {"x": x, "norm_g": norm_g, "w_in": w_in, "q_norm_g": q_norm_g, "k_norm_g": k_norm_g,
            "rel_bias": rel_bias, "conv_w": conv_w, "conv_b": conv_b, "conv_ln_g": conv_ln_g,
            "conv_ln_b": conv_ln_b, "w_out": w_out}


def reference(x, norm_g, w_in, q_norm_g, k_norm_g, rel_bias, conv_w, conv_b, conv_ln_g,
              conv_ln_b, w_out):
    B, S, _ = x.shape
    split_pts = list(np.cumsum(IN_SPLITS)[:-1])
    for l in range(DEPTH):
        xn = rms_norm(x, norm_g[l])
        h = jnp.einsum('bsd,de->bse', xn, w_in[l])
        c_val, c_glu, c_gate, q, k, v, a_gate = jnp.split(h, split_pts, axis=-1)

        conv_out = conformer_conv(c_val, c_glu, conv_w[l], conv_b[l], conv_ln_g[l], conv_ln_b[l])
        conv_out = conv_out * jax.nn.silu(c_gate)

        q = rms_norm(q.reshape(B, S, ATTN_HEADS, HEAD_DIM), q_norm_g[l])
        k = rms_norm(k.reshape(B, S, ATTN_HEADS, HEAD_DIM), k_norm_g[l])
        v = v.reshape(B, S, ATTN_HEADS, HEAD_DIM)
        attn = longnet_mixture(q, k, v, rel_bias).astype(x.dtype).reshape(B, S, ATTN_WIDTH)
        attn_out = attn * jax.nn.silu(a_gate)

        mixed = jnp.concatenate([conv_out, attn_out], axis=-1)
        x = x + jnp.einsum('bse,ed->bsd', mixed, w_out[l])
    return x
```

```python
import functools
import math

import jax
import jax.numpy as jnp
import numpy as np
from jax import lax
from jax.experimental import pallas as pl
from jax.experimental.pallas import tpu as pltpu

F32 = jnp.float32
BF16 = jnp.bfloat16

SEQ = 8192
D_MODEL = 4096
CONV_CH = 2048
HEADS = 16
HEAD_DIM = 128
CONV_K = 31
PATTERNS = ((128, 1), (512, 4), (2048, 16))
RADIUS = 64
REL_BUCKETS = 32
REL_MAX_DISTANCE = 1024
NORM_EPS = 1e-6
LN_EPS = 1e-5
NEG_INF = -1e30
IN_WIDTH = 3 * CONV_CH + 4 * HEADS * HEAD_DIM

LANES = 128
N_SLABS = IN_WIDTH // LANES
CONV_SLABS = CONV_CH // LANES
SLAB_VAL, SLAB_GLU, SLAB_CGATE, SLAB_Q, SLAB_K, SLAB_V, SLAB_AGATE = (
    0, 16, 32, 48, 64, 80, 96)

VMEM_LIMIT = 56 * 1024 * 1024

QBLK = 128
KWIN = QBLK + 2 * RADIUS


def _sigmoid(x):
    return 1.0 / (1.0 + jnp.exp(-x))


RN_ROWS = 256


def _rmsnorm_kernel(x_ref, g_ref, o_ref):
    x = x_ref[...]
    ms = jnp.mean(x * x, axis=-1, keepdims=True)
    o_ref[...] = (x * lax.rsqrt(ms + NORM_EPS) * g_ref[...]).astype(o_ref.dtype)


def _rmsnorm(x, g):
    return pl.pallas_call(
        _rmsnorm_kernel,
        out_shape=jax.ShapeDtypeStruct((SEQ, D_MODEL), BF16),
        grid=(SEQ // RN_ROWS,),
        in_specs=[pl.BlockSpec((RN_ROWS, D_MODEL), lambda i: (i, 0)),
                  pl.BlockSpec((1, D_MODEL), lambda i: (0, 0))],
        out_specs=pl.BlockSpec((RN_ROWS, D_MODEL), lambda i: (i, 0)),
        compiler_params=pltpu.CompilerParams(
            dimension_semantics=("arbitrary",), vmem_limit_bytes=VMEM_LIMIT),
        name="rmsnorm",
    )(x, g)


IP_BM = 1024
IP_BN = 1024


def _in_proj_kernel(x_ref, w_ref, o_ref):
    acc = jnp.dot(x_ref[...], w_ref[...], preferred_element_type=F32)
    for c in range(IP_BN // LANES):
        o_ref[c] = acc[:, c * LANES:(c + 1) * LANES].astype(o_ref.dtype)


def _in_proj(xn, w):
    return pl.pallas_call(
        _in_proj_kernel,
        out_shape=jax.ShapeDtypeStruct((N_SLABS, SEQ, LANES), BF16),
        grid=(SEQ // IP_BM, IN_WIDTH // IP_BN),
        in_specs=[pl.BlockSpec((IP_BM, D_MODEL), lambda i, j: (i, 0)),
                  pl.BlockSpec((D_MODEL, IP_BN), lambda i, j: (0, j))],
        out_specs=pl.BlockSpec((IP_BN // LANES, IP_BM, LANES), lambda i, j: (j, i, 0)),
        compiler_params=pltpu.CompilerParams(
            dimension_semantics=("arbitrary", "arbitrary"), vmem_limit_bytes=VMEM_LIMIT),
        name="in_proj",
    )(xn, w)


CV_T = 512
CV_HALO = 16
CV_R = 64


def _conv_kernel(val_ref, glu_ref, gate_ref, valp_ref, glup_ref, valn_ref, glun_ref,
                 w_ref, b_ref, lng_ref, lnb_ref, o_ref, a_s, c_s):
    t = pl.program_id(0)
    nt = pl.num_programs(0)

    def glu(v, g):
        return v.astype(F32) * _sigmoid(g.astype(F32))

    def fill(c, carry):
        a_s[c, CV_HALO:CV_HALO + CV_T, :] = glu(val_ref[c], glu_ref[c])
        a_s[c, 0:CV_HALO, :] = jnp.where(t > 0, glu(valp_ref[c], glup_ref[c]), 0.0)
        a_s[c, CV_HALO + CV_T:, :] = jnp.where(t < nt - 1, glu(valn_ref[c], glun_ref[c]), 0.0)
        return carry

    lax.fori_loop(0, CONV_SLABS, fill, 0)

    def chunk(rc, carry):
        r0 = pl.multiple_of(rc * CV_R, CV_R)

        def slab(c, s1):
            acc = jnp.broadcast_to(b_ref[c], (CV_R, LANES))
            for j in range(CONV_K):
                off = j + CV_HALO - CONV_K // 2
                acc = acc + w_ref[c, j:j + 1, :] * a_s[c, pl.ds(r0 + off, CV_R), :]
            c_s[c] = acc
            return s1 + acc

        s1 = lax.fori_loop(0, CONV_SLABS, slab, jnp.zeros((CV_R, LANES), F32))
        mu = jnp.sum(s1, axis=-1, keepdims=True) * (1.0 / CONV_CH)

        def sq(c, s2):
            d = c_s[c] - mu
            return s2 + d * d

        s2 = lax.fori_loop(0, CONV_SLABS, sq, jnp.zeros((CV_R, LANES), F32))
        rstd = lax.rsqrt(jnp.sum(s2, axis=-1, keepdims=True) * (1.0 / CONV_CH) + LN_EPS)

        for c in range(CONV_SLABS):
            y = (c_s[c] - mu) * rstd * lng_ref[c] + lnb_ref[c]
            y = y * _sigmoid(y)
            g = gate_ref[c, pl.ds(r0, CV_R), :].astype(F32)
            y = y * (g * _sigmoid(g))
            o_ref[pl.ds(r0, CV_R), c * LANES:(c + 1) * LANES] = y.astype(o_ref.dtype)
        return carry

    lax.fori_loop(0, CV_T // CV_R, chunk, 0)


def _conv_branch(h_slabs, conv_w, conv_b, ln_g, ln_b):
    hb = CV_T // CV_HALO
    last = SEQ // CV_HALO - 1
    main = lambda s: pl.BlockSpec((CONV_SLABS, CV_T, LANES), lambda t: (s, t, 0))
    prev = lambda s: pl.BlockSpec((CONV_SLABS, CV_HALO, LANES),
                                  lambda t: (s, jnp.maximum(t * hb - 1, 0), 0))
    nxt = lambda s: pl.BlockSpec((CONV_SLABS, CV_HALO, LANES),
                                 lambda t: (s, jnp.minimum((t + 1) * hb, last), 0))
    full3 = lambda a: pl.BlockSpec(a.shape, lambda t: (0, 0, 0))
    w = conv_w.reshape(CONV_K, CONV_SLABS, LANES).transpose(1, 0, 2)
    b = conv_b.reshape(CONV_SLABS, 1, LANES)
    g = ln_g.reshape(CONV_SLABS, 1, LANES)
    be = ln_b.reshape(CONV_SLABS, 1, LANES)
    sv, sg, sc = (SLAB_VAL // CONV_SLABS, SLAB_GLU // CONV_SLABS, SLAB_CGATE // CONV_SLABS)
    return pl.pallas_call(
        _conv_kernel,
        out_shape=jax.ShapeDtypeStruct((SEQ, CONV_CH), BF16),
        grid=(SEQ // CV_T,),
        in_specs=[main(sv), main(sg), main(sc), prev(sv), prev(sg), nxt(sv), nxt(sg),
                  full3(w), full3(b), full3(g), full3(be)],
        out_specs=pl.BlockSpec((CV_T, CONV_CH), lambda t: (t, 0)),
        scratch_shapes=[pltpu.VMEM((CONV_SLABS, CV_T + 2 * CV_HALO, LANES), F32),
                        pltpu.VMEM((CONV_SLABS, CV_R, LANES), F32)],
        compiler_params=pltpu.CompilerParams(
            dimension_semantics=("arbitrary",), vmem_limit_bytes=VMEM_LIMIT),
        name="conv_branch",
    )(h_slabs, h_slabs, h_slabs, h_slabs, h_slabs, h_slabs, h_slabs, w, b, g, be)


N_VARIANTS = 3
ST_ROWS = 512
GA_ROWS = 256


def _t5_bucket_np(rel):
    half = REL_BUCKETS // 2
    exact = half // 2
    n = np.abs(rel)
    nf = np.maximum(n, 1).astype(np.float32)
    large = exact + (np.log(nf / np.float32(exact)) / np.float32(math.log(REL_MAX_DISTANCE / exact))
                     * np.float32(half - exact)).astype(np.int32)
    large = np.minimum(large, half - 1)
    return np.where(rel > 0, half, 0) + np.where(n < exact, n, large)


def _bucket_tiles():
    tiles = []
    q = np.arange(QBLK)[:, None]
    k = np.arange(KWIN)[None, :]
    for _, dil in PATTERNS:
        for v in range(N_VARIANTS):
            rel = k - v * RADIUS - q
            tiles.append(np.where(np.abs(rel) <= RADIUS, _t5_bucket_np(rel * dil), -1))
    return np.stack(tiles).astype(np.int32)


def _attn_kernel(rb_ref, q_ref, k_ref, v_ref, g_ref, gq_ref, gk_ref, idx_ref, o_ref,
                 qf, kf, vf, qg, kg, vg, bias_s, acc_s, lse_s):
    hh = pl.program_id(0)
    scale = HEAD_DIM ** -0.5

    for pv in range(len(PATTERNS) * N_VARIANTS):
        idx = idx_ref[pv]

        def pick(b, tile, idx=idx):
            return jnp.where(idx == b, rb_ref[b, hh], tile)

        bias_s[pv] = lax.fori_loop(0, REL_BUCKETS, pick,
                                   jnp.full((QBLK, KWIN), NEG_INF, F32))

    def stage(src_ref, dst, gain_ref):
        def body(ch, carry):
            rows = pl.ds(pl.multiple_of(ch * ST_ROWS, ST_ROWS), ST_ROWS)
            x = src_ref[0, rows, :].astype(F32)
            if gain_ref is not None:
                ms = jnp.mean(x * x, axis=-1, keepdims=True)
                x = x * lax.rsqrt(ms + NORM_EPS) * gain_ref[...]
            dst[rows, :] = x
            return carry
        lax.fori_loop(0, SEQ // ST_ROWS, body, 0)

    stage(q_ref, qf, gq_ref)
    stage(k_ref, kf, gk_ref)
    stage(v_ref, vf, None)

    for p, (_, dil) in enumerate(PATTERNS):
        sub_len = SEQ // dil
        nb = sub_len // QBLK
        nch = sub_len // GA_ROWS

        def gather(src, dst, dil=dil, sub_len=sub_len, nch=nch):
            def body(i, carry):
                r = i // nch
                ch = i % nch
                if dil == 1:
                    x = src[pl.ds(pl.multiple_of(ch * GA_ROWS, GA_ROWS), GA_ROWS), :]
                else:
                    x = src[pl.ds(r + ch * (GA_ROWS * dil), GA_ROWS, stride=dil), :]
                d0 = pl.multiple_of(r * sub_len + ch * GA_ROWS, GA_ROWS)
                dst[pl.ds(d0, GA_ROWS), :] = x.astype(BF16)
                return carry
            lax.fori_loop(0, SEQ // GA_ROWS, body, 0)

        gather(qf, qg)
        gather(kf, kg)
        gather(vf, vg)

        def block(i, carry, p=p, dil=dil, sub_len=sub_len, nb=nb):
            r = i // nb
            b = i % nb
            q0 = pl.multiple_of(i * QBLK, QBLK)
            variant = jnp.where(b == 0, 0, jnp.where(b == nb - 1, 2, 1))
            k0 = pl.multiple_of(r * sub_len + b * QBLK - variant * RADIUS, RADIUS)
            q = qg[pl.ds(q0, QBLK), :]
            kw = kg[pl.ds(k0, KWIN), :]
            vw = vg[pl.ds(k0, KWIN), :]
            s = lax.dot_general(q, kw, (((1,), (1,)), ((), ())), preferred_element_type=F32)
            s = s * scale + bias_s[p * N_VARIANTS + variant]
            m = jnp.max(s, axis=-1, keepdims=True)
            e = jnp.exp(s - m)
            l = jnp.sum(e, axis=-1, keepdims=True)
            o = jnp.dot(e.astype(BF16), vw, preferred_element_type=F32) / l
            lse = jnp.broadcast_to(m + jnp.log(l), (QBLK, LANES))
            if p == 0:
                acc_s[pl.ds(q0, QBLK), :] = o
                lse_s[pl.ds(q0, QBLK), :] = lse
            else:
                rows = pl.ds(dil * QBLK * b + r, QBLK, stride=dil)
                o_old = acc_s[rows, :]
                l_old = lse_s[rows, :]
                mx = jnp.maximum(l_old, lse)
                tot = mx + jnp.log(jnp.exp(l_old - mx) + jnp.exp(lse - mx))
                acc_s[rows, :] = o_old * jnp.exp(l_old - tot) + o * jnp.exp(lse - tot)
                lse_s[rows, :] = tot
            return carry

        lax.fori_loop(0, SEQ // QBLK, block, 0)

    def finish(ch, carry):
        rows = pl.ds(pl.multiple_of(ch * ST_ROWS, ST_ROWS), ST_ROWS)
        g = g_ref[0, rows, :].astype(F32)
        o_ref[rows, :] = (acc_s[rows, :] * (g * _sigmoid(g))).astype(o_ref.dtype)
        return carry

    lax.fori_loop(0, SEQ // ST_ROWS, finish, 0)


def _attention(h_slabs, q_gain, k_gain, rel_bias):
    slab = lambda s0: pl.BlockSpec((1, SEQ, LANES), lambda h: (s0 + h, 0, 0))
    idx = jnp.asarray(_bucket_tiles())
    n_tiles = idx.shape[0]
    return pl.pallas_call(
        _attn_kernel,
        out_shape=jax.ShapeDtypeStruct((SEQ, HEADS * HEAD_DIM), BF16),
        grid=(HEADS,),
        in_specs=[pl.BlockSpec(memory_space=pltpu.SMEM),
                  slab(SLAB_Q), slab(SLAB_K), slab(SLAB_V), slab(SLAB_AGATE),
                  pl.BlockSpec((1, HEAD_DIM), lambda h: (0, 0)),
                  pl.BlockSpec((1, HEAD_DIM), lambda h: (0, 0)),
                  pl.BlockSpec((n_tiles, QBLK, KWIN), lambda h: (0, 0, 0))],
        out_specs=pl.BlockSpec((SEQ, LANES), lambda h: (0, h)),
        scratch_shapes=[pltpu.VMEM((SEQ, LANES), F32), pltpu.VMEM((SEQ, LANES), F32),
                        pltpu.VMEM((SEQ, LANES), F32),
                        pltpu.VMEM((SEQ, LANES), BF16), pltpu.VMEM((SEQ, LANES), BF16),
                        pltpu.VMEM((SEQ, LANES), BF16),
                        pltpu.VMEM((n_tiles, QBLK, KWIN), F32),
                        pltpu.VMEM((SEQ, LANES), F32), pltpu.VMEM((SEQ, LANES), F32)],
        compiler_params=pltpu.CompilerParams(
            dimension_semantics=("arbitrary",), vmem_limit_bytes=VMEM_LIMIT),
        name="dilated_attn",
    )(rel_bias, h_slabs, h_slabs, h_slabs, h_slabs,
      q_gain.reshape(1, HEAD_DIM), k_gain.reshape(1, HEAD_DIM), idx)


OP_BM = 1024
OP_BN = 512


def _out_proj_kernel(c_ref, a_ref, w_ref, x_ref, o_ref):
    acc = jnp.dot(c_ref[...], w_ref[0:CONV_CH, :], preferred_element_type=F32)
    acc = acc + jnp.dot(a_ref[...], w_ref[CONV_CH:, :], preferred_element_type=F32)
    o_ref[...] = x_ref[...] + acc


def _out_proj(conv_out, attn_out, w, x):
    return pl.pallas_call(
        _out_proj_kernel,
        out_shape=jax.ShapeDtypeStruct((SEQ, D_MODEL), F32),
        grid=(SEQ // OP_BM, D_MODEL // OP_BN),
        in_specs=[pl.BlockSpec((OP_BM, CONV_CH), lambda i, j: (i, 0)),
                  pl.BlockSpec((OP_BM, HEADS * HEAD_DIM), lambda i, j: (i, 0)),
                  pl.BlockSpec((D_MODEL, OP_BN), lambda i, j: (0, j)),
                  pl.BlockSpec((OP_BM, OP_BN), lambda i, j: (i, j))],
        out_specs=pl.BlockSpec((OP_BM, OP_BN), lambda i, j: (i, j)),
        compiler_params=pltpu.CompilerParams(
            dimension_semantics=("arbitrary", "arbitrary"), vmem_limit_bytes=VMEM_LIMIT),
        name="out_proj",
    )(conv_out, attn_out, w, x)


def kernel(x, norm_g, w_in, q_norm_g, k_norm_g, rel_bias, conv_w, conv_b, conv_ln_g,
           conv_ln_b, w_out):
    batch, seq, d_model = x.shape
    assert (batch, seq, d_model) == (1, SEQ, D_MODEL)
    depth = norm_g.shape[0]
    y = x.reshape(SEQ, D_MODEL)
    for l in range(depth):
        xn = _rmsnorm(y, norm_g[l].reshape(1, D_MODEL))
        h = _in_proj(xn, w_in[l].astype(BF16))
        conv_out = _conv_branch(h, conv_w[l], conv_b[l], conv_ln_g[l], conv_ln_b[l])
        attn_out = _attention(h, q_norm_g[l], k_norm_g[l], rel_bias)
        y = _out_proj(conv_out, attn_out, w_out[l].astype(BF16), y)
    return y.reshape(batch, seq, d_model)
```

```python
import functools
import math

import jax
import jax.numpy as jnp
import numpy as np
from jax import lax
from jax.experimental import pallas as pl
from jax.experimental.pallas import tpu as pltpu

F32 = jnp.float32
BF16 = jnp.bfloat16

SEQ = 8192
D_MODEL = 4096
CONV_CH = 2048
HEADS = 16
HEAD_DIM = 128
CONV_K = 31
PATTERNS = ((128, 1), (512, 4), (2048, 16))
RADIUS = 64
REL_BUCKETS = 32
REL_MAX_DISTANCE = 1024
NORM_EPS = 1e-6
LN_EPS = 1e-5
NEG_INF = -1e30
IN_WIDTH = 3 * CONV_CH + 4 * HEADS * HEAD_DIM

LANES = 128
N_SLABS = IN_WIDTH // LANES
CONV_SLABS = CONV_CH // LANES
SLAB_VAL, SLAB_GLU, SLAB_CGATE, SLAB_Q, SLAB_K, SLAB_V, SLAB_AGATE = (
    0, 16, 32, 48, 64, 80, 96)

VMEM_LIMIT = 56 * 1024 * 1024

QBLK = 128
KWIN = QBLK + 2 * RADIUS


def _sigmoid(x):
    return 1.0 / (1.0 + jnp.exp(-x))


RN_ROWS = 256


def _rmsnorm_kernel(x_ref, g_ref, o_ref):
    x = x_ref[...]
    ms = jnp.mean(x * x, axis=-1, keepdims=True)
    o_ref[...] = (x * lax.rsqrt(ms + NORM_EPS) * g_ref[...]).astype(o_ref.dtype)


def _rmsnorm(x, g):
    return pl.pallas_call(
        _rmsnorm_kernel,
        out_shape=jax.ShapeDtypeStruct((SEQ, D_MODEL), BF16),
        grid=(SEQ // RN_ROWS,),
        in_specs=[pl.BlockSpec((RN_ROWS, D_MODEL), lambda i: (i, 0)),
                  pl.BlockSpec((1, D_MODEL), lambda i: (0, 0))],
        out_specs=pl.BlockSpec((RN_ROWS, D_MODEL), lambda i: (i, 0)),
        compiler_params=pltpu.CompilerParams(
            dimension_semantics=("arbitrary",), vmem_limit_bytes=VMEM_LIMIT),
        name="rmsnorm",
    )(x, g)


IP_BM = 1024
IP_BN = 1024


def _in_proj_kernel(x_ref, w_ref, o_ref):
    acc = jnp.dot(x_ref[...], w_ref[...], preferred_element_type=F32)
    for c in range(IP_BN // LANES):
        o_ref[c] = acc[:, c * LANES:(c + 1) * LANES].astype(o_ref.dtype)


def _in_proj(xn, w):
    return pl.pallas_call(
        _in_proj_kernel,
        out_shape=jax.ShapeDtypeStruct((N_SLABS, SEQ, LANES), BF16),
        grid=(SEQ // IP_BM, IN_WIDTH // IP_BN),
        in_specs=[pl.BlockSpec((IP_BM, D_MODEL), lambda i, j: (i, 0)),
                  pl.BlockSpec((D_MODEL, IP_BN), lambda i, j: (0, j))],
        out_specs=pl.BlockSpec((IP_BN // LANES, IP_BM, LANES), lambda i, j: (j, i, 0)),
        compiler_params=pltpu.CompilerParams(
            dimension_semantics=("arbitrary", "arbitrary"), vmem_limit_bytes=VMEM_LIMIT),
        name="in_proj",
    )(xn, w)


CV_T = 512
CV_HALO = 16
CV_R = 64


def _conv_kernel(val_ref, glu_ref, gate_ref, valp_ref, glup_ref, valn_ref, glun_ref,
                 w_ref, b_ref, lng_ref, lnb_ref, o_ref, a_s, c_s):
    t = pl.program_id(0)
    nt = pl.num_programs(0)

    def glu(v, g):
        return v.astype(F32) * _sigmoid(g.astype(F32))

    def fill(c, carry):
        a_s[c, CV_HALO:CV_HALO + CV_T, :] = glu(val_ref[c], glu_ref[c])
        a_s[c, 0:CV_HALO, :] = jnp.where(t > 0, glu(valp_ref[c], glup_ref[c]), 0.0)
        a_s[c, CV_HALO + CV_T:, :] = jnp.where(t < nt - 1, glu(valn_ref[c], glun_ref[c]), 0.0)
        return carry

    lax.fori_loop(0, CONV_SLABS, fill, 0)

    def chunk(rc, carry):
        r0 = pl.multiple_of(rc * CV_R, CV_R)

        def slab(c, s1):
            acc = jnp.broadcast_to(b_ref[c], (CV_R, LANES))
            for j in range(CONV_K):
                off = j + CV_HALO - CONV_K // 2
                acc = acc + w_ref[c, j:j + 1, :] * a_s[c, pl.ds(r0 + off, CV_R), :]
            c_s[c] = acc
            return s1 + acc

        s1 = lax.fori_loop(0, CONV_SLABS, slab, jnp.zeros((CV_R, LANES), F32))
        mu = jnp.sum(s1, axis=-1, keepdims=True) * (1.0 / CONV_CH)

        def sq(c, s2):
            d = c_s[c] - mu
            return s2 + d * d

        s2 = lax.fori_loop(0, CONV_SLABS, sq, jnp.zeros((CV_R, LANES), F32))
        rstd = lax.rsqrt(jnp.sum(s2, axis=-1, keepdims=True) * (1.0 / CONV_CH) + LN_EPS)

        for c in range(CONV_SLABS):
            y = (c_s[c] - mu) * rstd * lng_ref[c] + lnb_ref[c]
            y = y * _sigmoid(y)
            g = gate_ref[c, pl.ds(r0, CV_R), :].astype(F32)
            y = y * (g * _sigmoid(g))
            o_ref[pl.ds(r0, CV_R), c * LANES:(c + 1) * LANES] = y.astype(o_ref.dtype)
        return carry

    lax.fori_loop(0, CV_T // CV_R, chunk, 0)


def _conv_branch(h_slabs, conv_w, conv_b, ln_g, ln_b):
    hb = CV_T // CV_HALO
    last = SEQ // CV_HALO - 1
    main = lambda s: pl.BlockSpec((CONV_SLABS, CV_T, LANES), lambda t: (s, t, 0))
    prev = lambda s: pl.BlockSpec((CONV_SLABS, CV_HALO, LANES),
                                  lambda t: (s, jnp.maximum(t * hb - 1, 0), 0))
    nxt = lambda s: pl.BlockSpec((CONV_SLABS, CV_HALO, LANES),
                                 lambda t: (s, jnp.minimum((t + 1) * hb, last), 0))
    full3 = lambda a: pl.BlockSpec(a.shape, lambda t: (0, 0, 0))
    w = conv_w.reshape(CONV_K, CONV_SLABS, LANES).transpose(1, 0, 2)
    b = conv_b.reshape(CONV_SLABS, 1, LANES)
    g = ln_g.reshape(CONV_SLABS, 1, LANES)
    be = ln_b.reshape(CONV_SLABS, 1, LANES)
    sv, sg, sc = (SLAB_VAL // CONV_SLABS, SLAB_GLU // CONV_SLABS, SLAB_CGATE // CONV_SLABS)
    return pl.pallas_call(
        _conv_kernel,
        out_shape=jax.ShapeDtypeStruct((SEQ, CONV_CH), BF16),
        grid=(SEQ // CV_T,),
        in_specs=[main(sv), main(sg), main(sc), prev(sv), prev(sg), nxt(sv), nxt(sg),
                  full3(w), full3(b), full3(g), full3(be)],
        out_specs=pl.BlockSpec((CV_T, CONV_CH), lambda t: (t, 0)),
        scratch_shapes=[pltpu.VMEM((CONV_SLABS, CV_T + 2 * CV_HALO, LANES), F32),
                        pltpu.VMEM((CONV_SLABS, CV_R, LANES), F32)],
        compiler_params=pltpu.CompilerParams(
            dimension_semantics=("arbitrary",), vmem_limit_bytes=VMEM_LIMIT),
        name="conv_branch",
    )(h_slabs, h_slabs, h_slabs, h_slabs, h_slabs, h_slabs, h_slabs, w, b, g, be)


N_VARIANTS = 3
ST_ROWS = 512
GA_ROWS = 256
ATT_GROUP = 8


def _t5_bucket_np(rel):
    half = REL_BUCKETS // 2
    exact = half // 2
    n = np.abs(rel)
    nf = np.maximum(n, 1).astype(np.float32)
    large = exact + (np.log(nf / np.float32(exact)) / np.float32(math.log(REL_MAX_DISTANCE / exact))
                     * np.float32(half - exact)).astype(np.int32)
    large = np.minimum(large, half - 1)
    return np.where(rel > 0, half, 0) + np.where(n < exact, n, large)


def _bucket_tiles():
    tiles = []
    q = np.arange(QBLK)[:, None]
    k = np.arange(KWIN)[None, :]
    for _, dil in PATTERNS:
        for v in range(N_VARIANTS):
            rel = k - v * RADIUS - q
            tiles.append(np.where(np.abs(rel) <= RADIUS, _t5_bucket_np(rel * dil), -1))
    return np.stack(tiles).astype(np.int32)


def _attn_kernel(rb_ref, q_ref, k_ref, v_ref, g_ref, gq_ref, gk_ref, idx_ref, o_ref,
                 qf, kf, vf, qg, kg, vg, bias_s, acc_s, lse_s):
    hh = pl.program_id(0)
    scale = HEAD_DIM ** -0.5

    for pv in range(len(PATTERNS) * N_VARIANTS):
        idx = idx_ref[pv]

        def pick(b, tile, idx=idx):
            return jnp.where(idx == b, rb_ref[b, hh], tile)

        bias_s[pv] = lax.fori_loop(0, REL_BUCKETS, pick,
                                   jnp.full((QBLK, KWIN), NEG_INF, F32))

    def stage(src_ref, dst, gain_ref):
        def body(ch, carry):
            rows = pl.ds(pl.multiple_of(ch * ST_ROWS, ST_ROWS), ST_ROWS)
            x = src_ref[0, rows, :].astype(F32)
            if gain_ref is not None:
                ms = jnp.mean(x * x, axis=-1, keepdims=True)
                x = x * lax.rsqrt(ms + NORM_EPS) * gain_ref[...]
            dst[rows, :] = x
            return carry
        lax.fori_loop(0, SEQ // ST_ROWS, body, 0)

    stage(q_ref, qf, gq_ref)
    stage(k_ref, kf, gk_ref)
    stage(v_ref, vf, None)

    for p, (_, dil) in enumerate(PATTERNS):
        sub_len = SEQ // dil
        nb = sub_len // QBLK
        nch = sub_len // GA_ROWS

        def gather(src, dst, dil=dil, sub_len=sub_len, nch=nch):
            def body(i, carry):
                r = i // nch
                ch = i % nch
                if dil == 1:
                    x = src[pl.ds(pl.multiple_of(ch * GA_ROWS, GA_ROWS), GA_ROWS), :]
                else:
                    x = src[pl.ds(r + ch * (GA_ROWS * dil), GA_ROWS, stride=dil), :]
                d0 = pl.multiple_of(r * sub_len + ch * GA_ROWS, GA_ROWS)
                dst[pl.ds(d0, GA_ROWS), :] = x.astype(BF16)
                return carry
            lax.fori_loop(0, SEQ // GA_ROWS, body, 0)

        gather(qf, qg)
        gather(kf, kg)
        gather(vf, vg)

        def group(gi, carry, p=p, dil=dil, sub_len=sub_len, nb=nb):
            blocks = []
            for u in range(ATT_GROUP):
                i = gi * ATT_GROUP + u
                r = i // nb
                b = i % nb
                q0 = pl.multiple_of(i * QBLK, QBLK)
                variant = jnp.where(b == 0, 0, jnp.where(b == nb - 1, 2, 1))
                k0 = pl.multiple_of(r * sub_len + b * QBLK - variant * RADIUS, RADIUS)
                if p == 0:
                    rows = pl.ds(q0, QBLK)
                else:
                    rows = pl.ds(dil * QBLK * b + r, QBLK, stride=dil)
                blocks.append((q0, k0, variant, rows))

            scores = []
            for q0, k0, variant, _ in blocks:
                s = lax.dot_general(qg[pl.ds(q0, QBLK), :], kg[pl.ds(k0, KWIN), :],
                                    (((1,), (1,)), ((), ())), preferred_element_type=F32)
                scores.append(s * scale + bias_s[p * N_VARIANTS + variant])

            probs = []
            for s in scores:
                m = jnp.max(s, axis=-1, keepdims=True)
                e = jnp.exp(s - m)
                l = jnp.sum(e, axis=-1, keepdims=True)
                probs.append((e.astype(BF16), l, m))

            for (_, k0, _, rows), (e, l, m) in zip(blocks, probs):
                o = jnp.dot(e, vg[pl.ds(k0, KWIN), :], preferred_element_type=F32) / l
                lse = jnp.broadcast_to(m + jnp.log(l), (QBLK, LANES))
                if p == 0:
                    acc_s[rows, :] = o
                    lse_s[rows, :] = lse
                else:
                    o_old = acc_s[rows, :]
                    l_old = lse_s[rows, :]
                    mx = jnp.maximum(l_old, lse)
                    tot = mx + jnp.log(jnp.exp(l_old - mx) + jnp.exp(lse - mx))
                    acc_s[rows, :] = o_old * jnp.exp(l_old - tot) + o * jnp.exp(lse - tot)
                    lse_s[rows, :] = tot
            return carry

        lax.fori_loop(0, SEQ // QBLK // ATT_GROUP, group, 0)

    def finish(ch, carry):
        rows = pl.ds(pl.multiple_of(ch * ST_ROWS, ST_ROWS), ST_ROWS)
        g = g_ref[0, rows, :].astype(F32)
        o_ref[rows, :] = (acc_s[rows, :] * (g * _sigmoid(g))).astype(o_ref.dtype)
        return carry

    lax.fori_loop(0, SEQ // ST_ROWS, finish, 0)


def _attention(h_slabs, q_gain, k_gain, rel_bias):
    slab = lambda s0: pl.BlockSpec((1, SEQ, LANES), lambda h: (s0 + h, 0, 0))
    idx = jnp.asarray(_bucket_tiles())
    n_tiles = idx.shape[0]
    return pl.pallas_call(
        _attn_kernel,
        out_shape=jax.ShapeDtypeStruct((SEQ, HEADS * HEAD_DIM), BF16),
        grid=(HEADS,),
        in_specs=[pl.BlockSpec(memory_space=pltpu.SMEM),
                  slab(SLAB_Q), slab(SLAB_K), slab(SLAB_V), slab(SLAB_AGATE),
                  pl.BlockSpec((1, HEAD_DIM), lambda h: (0, 0)),
                  pl.BlockSpec((1, HEAD_DIM), lambda h: (0, 0)),
                  pl.BlockSpec((n_tiles, QBLK, KWIN), lambda h: (0, 0, 0))],
        out_specs=pl.BlockSpec((SEQ, LANES), lambda h: (0, h)),
        scratch_shapes=[pltpu.VMEM((SEQ, LANES), F32), pltpu.VMEM((SEQ, LANES), F32),
                        pltpu.VMEM((SEQ, LANES), F32),
                        pltpu.VMEM((SEQ, LANES), BF16), pltpu.VMEM((SEQ, LANES), BF16),
                        pltpu.VMEM((SEQ, LANES), BF16),
                        pltpu.VMEM((n_tiles, QBLK, KWIN), F32),
                        pltpu.VMEM((SEQ, LANES), F32), pltpu.VMEM((SEQ, LANES), F32)],
        compiler_params=pltpu.CompilerParams(
            dimension_semantics=("arbitrary",), vmem_limit_bytes=VMEM_LIMIT),
        name="dilated_attn",
    )(rel_bias, h_slabs, h_slabs, h_slabs, h_slabs,
      q_gain.reshape(1, HEAD_DIM), k_gain.reshape(1, HEAD_DIM), idx)


OP_BM = 1024
OP_BN = 512


def _out_proj_kernel(c_ref, a_ref, w_ref, x_ref, o_ref):
    acc = jnp.dot(c_ref[...], w_ref[0:CONV_CH, :], preferred_element_type=F32)
    acc = acc + jnp.dot(a_ref[...], w_ref[CONV_CH:, :], preferred_element_type=F32)
    o_ref[...] = x_ref[...] + acc


def _out_proj(conv_out, attn_out, w, x):
    return pl.pallas_call(
        _out_proj_kernel,
        out_shape=jax.ShapeDtypeStruct((SEQ, D_MODEL), F32),
        grid=(SEQ // OP_BM, D_MODEL // OP_BN),
        in_specs=[pl.BlockSpec((OP_BM, CONV_CH), lambda i, j: (i, 0)),
                  pl.BlockSpec((OP_BM, HEADS * HEAD_DIM), lambda i, j: (i, 0)),
                  pl.BlockSpec((D_MODEL, OP_BN), lambda i, j: (0, j)),
                  pl.BlockSpec((OP_BM, OP_BN), lambda i, j: (i, j))],
        out_specs=pl.BlockSpec((OP_BM, OP_BN), lambda i, j: (i, j)),
        compiler_params=pltpu.CompilerParams(
            dimension_semantics=("arbitrary", "arbitrary"), vmem_limit_bytes=VMEM_LIMIT),
        name="out_proj",
    )(conv_out, attn_out, w, x)


def kernel(x, norm_g, w_in, q_norm_g, k_norm_g, rel_bias, conv_w, conv_b, conv_ln_g,
           conv_ln_b, w_out):
    batch, seq, d_model = x.shape
    assert (batch, seq, d_model) == (1, SEQ, D_MODEL)
    depth = norm_g.shape[0]
    y = x.reshape(SEQ, D_MODEL)
    for l in range(depth):
        xn = _rmsnorm(y, norm_g[l].reshape(1, D_MODEL))
        h = _in_proj(xn, w_in[l].astype(BF16))
        conv_out = _conv_branch(h, conv_w[l], conv_b[l], conv_ln_g[l], conv_ln_b[l])
        attn_out = _attention(h, q_norm_g[l], k_norm_g[l], rel_bias)
        y = _out_proj(conv_out, attn_out, w_out[l].astype(BF16), y)
    return y.reshape(batch, seq, d_model)
```

```python
import functools
import math

import jax
import jax.numpy as jnp
import numpy as np
from jax import lax
from jax.experimental import pallas as pl
from jax.experimental.pallas import tpu as pltpu

F32 = jnp.float32
BF16 = jnp.bfloat16

SEQ = 8192
D_MODEL = 4096
CONV_CH = 2048
HEADS = 16
HEAD_DIM = 128
CONV_K = 31
PATTERNS = ((128, 1), (512, 4), (2048, 16))
RADIUS = 64
REL_BUCKETS = 32
REL_MAX_DISTANCE = 1024
NORM_EPS = 1e-6
LN_EPS = 1e-5
NEG_INF = -1e30
IN_WIDTH = 3 * CONV_CH + 4 * HEADS * HEAD_DIM

LANES = 128
N_SLABS = IN_WIDTH // LANES
CONV_SLABS = CONV_CH // LANES
SLAB_VAL, SLAB_GLU, SLAB_CGATE, SLAB_Q, SLAB_K, SLAB_V, SLAB_AGATE = (
    0, 16, 32, 48, 64, 80, 96)

VMEM_LIMIT = 56 * 1024 * 1024

QBLK = 128
KWIN = QBLK + 2 * RADIUS


def _sigmoid(x):
    return 1.0 / (1.0 + jnp.exp(-x))


RN_ROWS = 256


def _rmsnorm_kernel(x_ref, g_ref, o_ref):
    x = x_ref[...]
    ms = jnp.mean(x * x, axis=-1, keepdims=True)
    o_ref[...] = (x * lax.rsqrt(ms + NORM_EPS) * g_ref[...]).astype(o_ref.dtype)


def _rmsnorm(x, g):
    return pl.pallas_call(
        _rmsnorm_kernel,
        out_shape=jax.ShapeDtypeStruct((SEQ, D_MODEL), BF16),
        grid=(SEQ // RN_ROWS,),
        in_specs=[pl.BlockSpec((RN_ROWS, D_MODEL), lambda i: (i, 0)),
                  pl.BlockSpec((1, D_MODEL), lambda i: (0, 0))],
        out_specs=pl.BlockSpec((RN_ROWS, D_MODEL), lambda i: (i, 0)),
        compiler_params=pltpu.CompilerParams(
            dimension_semantics=("arbitrary",), vmem_limit_bytes=VMEM_LIMIT),
        name="rmsnorm",
    )(x, g)


IP_BM = 1024
IP_BN = 1024


def _in_proj_kernel(x_ref, w_ref, o_ref):
    acc = jnp.dot(x_ref[...], w_ref[...], preferred_element_type=F32)
    for c in range(IP_BN // LANES):
        o_ref[c] = acc[:, c * LANES:(c + 1) * LANES].astype(o_ref.dtype)


def _in_proj(xn, w):
    return pl.pallas_call(
        _in_proj_kernel,
        out_shape=jax.ShapeDtypeStruct((N_SLABS, SEQ, LANES), BF16),
        grid=(SEQ // IP_BM, IN_WIDTH // IP_BN),
        in_specs=[pl.BlockSpec((IP_BM, D_MODEL), lambda i, j: (i, 0)),
                  pl.BlockSpec((D_MODEL, IP_BN), lambda i, j: (0, j))],
        out_specs=pl.BlockSpec((IP_BN // LANES, IP_BM, LANES), lambda i, j: (j, i, 0)),
        compiler_params=pltpu.CompilerParams(
            dimension_semantics=("arbitrary", "arbitrary"), vmem_limit_bytes=VMEM_LIMIT),
        name="in_proj",
    )(xn, w)


CV_T = 512
CV_HALO = 16
CV_R = 128


def _conv_kernel(val_ref, glu_ref, gate_ref, valp_ref, glup_ref, valn_ref, glun_ref,
                 w_ref, b_ref, lng_ref, lnb_ref, o_ref, a_s, c_s):
    t = pl.program_id(0)
    nt = pl.num_programs(0)

    def glu(v, g):
        return v.astype(F32) * _sigmoid(g.astype(F32))

    def fill(c, carry):
        a_s[c, CV_HALO:CV_HALO + CV_T, :] = glu(val_ref[c], glu_ref[c])
        a_s[c, 0:CV_HALO, :] = jnp.where(t > 0, glu(valp_ref[c], glup_ref[c]), 0.0)
        a_s[c, CV_HALO + CV_T:, :] = jnp.where(t < nt - 1, glu(valn_ref[c], glun_ref[c]), 0.0)
        return carry

    lax.fori_loop(0, CONV_SLABS, fill, 0)

    def chunk(rc, carry):
        r0 = pl.multiple_of(rc * CV_R, CV_R)

        def slab(c, s1):
            acc = jnp.broadcast_to(b_ref[c], (CV_R, LANES))
            for j in range(CONV_K):
                off = j + CV_HALO - CONV_K // 2
                acc = acc + w_ref[c, j:j + 1, :] * a_s[c, pl.ds(r0 + off, CV_R), :]
            c_s[c] = acc
            return s1 + acc

        s1 = lax.fori_loop(0, CONV_SLABS, slab, jnp.zeros((CV_R, LANES), F32))
        mu = jnp.sum(s1, axis=-1, keepdims=True) * (1.0 / CONV_CH)

        def sq(c, s2):
            d = c_s[c] - mu
            return s2 + d * d

        s2 = lax.fori_loop(0, CONV_SLABS, sq, jnp.zeros((CV_R, LANES), F32))
        rstd = lax.rsqrt(jnp.sum(s2, axis=-1, keepdims=True) * (1.0 / CONV_CH) + LN_EPS)

        for c in range(CONV_SLABS):
            y = (c_s[c] - mu) * rstd * lng_ref[c] + lnb_ref[c]
            y = y * _sigmoid(y)
            g = gate_ref[c, pl.ds(r0, CV_R), :].astype(F32)
            y = y * (g * _sigmoid(g))
            o_ref[pl.ds(r0, CV_R), c * LANES:(c + 1) * LANES] = y.astype(o_ref.dtype)
        return carry

    lax.fori_loop(0, CV_T // CV_R, chunk, 0)


def _conv_branch(h_slabs, conv_w, conv_b, ln_g, ln_b):
    hb = CV_T // CV_HALO
    last = SEQ // CV_HALO - 1
    main = lambda s: pl.BlockSpec((CONV_SLABS, CV_T, LANES), lambda t: (s, t, 0))
    prev = lambda s: pl.BlockSpec((CONV_SLABS, CV_HALO, LANES),
                                  lambda t: (s, jnp.maximum(t * hb - 1, 0), 0))
    nxt = lambda s: pl.BlockSpec((CONV_SLABS, CV_HALO, LANES),
                                 lambda t: (s, jnp.minimum((t + 1) * hb, last), 0))
    full3 = lambda a: pl.BlockSpec(a.shape, lambda t: (0, 0, 0))
    w = conv_w.reshape(CONV_K, CONV_SLABS, LANES).transpose(1, 0, 2)
    b = conv_b.reshape(CONV_SLABS, 1, LANES)
    g = ln_g.reshape(CONV_SLABS, 1, LANES)
    be = ln_b.reshape(CONV_SLABS, 1, LANES)
    sv, sg, sc = (SLAB_VAL // CONV_SLABS, SLAB_GLU // CONV_SLABS, SLAB_CGATE // CONV_SLABS)
    return pl.pallas_call(
        _conv_kernel,
        out_shape=jax.ShapeDtypeStruct((SEQ, CONV_CH), BF16),
        grid=(SEQ // CV_T,),
        in_specs=[main(sv), main(sg), main(sc), prev(sv), prev(sg), nxt(sv), nxt(sg),
                  full3(w), full3(b), full3(g), full3(be)],
        out_specs=pl.BlockSpec((CV_T, CONV_CH), lambda t: (t, 0)),
        scratch_shapes=[pltpu.VMEM((CONV_SLABS, CV_T + 2 * CV_HALO, LANES), F32),
                        pltpu.VMEM((CONV_SLABS, CV_R, LANES), F32)],
        compiler_params=pltpu.CompilerParams(
            dimension_semantics=("arbitrary",), vmem_limit_bytes=VMEM_LIMIT),
        name="conv_branch",
    )(h_slabs, h_slabs, h_slabs, h_slabs, h_slabs, h_slabs, h_slabs, w, b, g, be)


N_VARIANTS = 3
ST_ROWS = 512
GA_ROWS = 256
ATT_GROUP = 8


def _t5_bucket_np(rel):
    half = REL_BUCKETS // 2
    exact = half // 2
    n = np.abs(rel)
    nf = np.maximum(n, 1).astype(np.float32)
    large = exact + (np.log(nf / np.float32(exact)) / np.float32(math.log(REL_MAX_DISTANCE / exact))
                     * np.float32(half - exact)).astype(np.int32)
    large = np.minimum(large, half - 1)
    return np.where(rel > 0, half, 0) + np.where(n < exact, n, large)


def _bucket_tiles():
    tiles = []
    q = np.arange(QBLK)[:, None]
    k = np.arange(KWIN)[None, :]
    for _, dil in PATTERNS:
        for v in range(N_VARIANTS):
            rel = k - v * RADIUS - q
            tiles.append(np.where(np.abs(rel) <= RADIUS, _t5_bucket_np(rel * dil), -1))
    return np.stack(tiles).astype(np.int32)


_TILE_BUCKETS = tuple(tuple(int(b) for b in np.unique(t) if b >= 0) for t in _bucket_tiles())


def _gather4_body(src, dst_f, dst_b, seg):
    n_seg = SEQ // seg
    chunks_per_class = seg // 4 // GA_ROWS

    def body(i, carry):
        cls = i // chunks_per_class
        ch = i % chunks_per_class
        g = cls % n_seg
        s = cls // n_seg
        x = src[pl.ds(g * seg + ch * (4 * GA_ROWS) + s, GA_ROWS, stride=4), :]
        d0 = pl.multiple_of(i * GA_ROWS, GA_ROWS)
        if dst_f is not None:
            dst_f[pl.ds(d0, GA_ROWS), :] = x
        dst_b[pl.ds(d0, GA_ROWS), :] = x.astype(BF16)
        return carry

    return body


def _attn_kernel(rb_ref, q_ref, k_ref, v_ref, g_ref, gq_ref, gk_ref, idx_ref, o_ref,
                 f0, f1, f2, f3, qg, kg, vg, bias_s, acc_s, lse_s):
    hh = pl.program_id(0)
    scale = HEAD_DIM ** -0.5

    for pv, buckets in enumerate(_TILE_BUCKETS):
        idx = idx_ref[pv]
        tile = jnp.full((QBLK, KWIN), NEG_INF, F32)
        for b in buckets:
            tile = jnp.where(idx == b, rb_ref[b, hh], tile)
        bias_s[pv] = tile

    def stage(ch, carry):
        rows = pl.ds(pl.multiple_of(ch * ST_ROWS, ST_ROWS), ST_ROWS)
        for src_ref, gain_ref, dst_f, dst_b in ((q_ref, gq_ref, f0, qg), (k_ref, gk_ref, f1, kg)):
            x = src_ref[0, rows, :].astype(F32)
            ms = jnp.mean(x * x, axis=-1, keepdims=True)
            x = x * lax.rsqrt(ms + NORM_EPS) * gain_ref[...]
            dst_f[rows, :] = x
            dst_b[rows, :] = x.astype(BF16)
        v = v_ref[0, rows, :]
        f2[rows, :] = v.astype(F32)
        vg[rows, :] = v
        return carry

    lax.fori_loop(0, SEQ // ST_ROWS, stage, 0)

    for p, (_, dil) in enumerate(PATTERNS):
        sub_len = SEQ // dil
        nb = sub_len // QBLK

        if p == 1:
            for src, dst_f, dst_b in ((f0, f3, qg), (f1, f0, kg), (f2, f1, vg)):
                lax.fori_loop(0, SEQ // GA_ROWS, _gather4_body(src, dst_f, dst_b, SEQ), 0)
        elif p == 2:
            for src, dst_b in ((f3, qg), (f0, kg), (f1, vg)):
                lax.fori_loop(0, SEQ // GA_ROWS, _gather4_body(src, None, dst_b, SEQ // 4), 0)

        def group(gi, carry, p=p, dil=dil, sub_len=sub_len, nb=nb):
            blocks = []
            for u in range(ATT_GROUP):
                i = gi * ATT_GROUP + u
                r = i // nb
                b = i % nb
                q0 = pl.multiple_of(i * QBLK, QBLK)
                variant = jnp.where(b == 0, 0, jnp.where(b == nb - 1, 2, 1))
                k0 = pl.multiple_of(r * sub_len + b * QBLK - variant * RADIUS, RADIUS)
                if p == 0:
                    rows = pl.ds(q0, QBLK)
                else:
                    rows = pl.ds(dil * QBLK * b + r, QBLK, stride=dil)
                blocks.append((q0, k0, variant, rows))

            scores = []
            for q0, k0, variant, _ in blocks:
                s = lax.dot_general(qg[pl.ds(q0, QBLK), :], kg[pl.ds(k0, KWIN), :],
                                    (((1,), (1,)), ((), ())), preferred_element_type=F32)
                scores.append(s * scale + bias_s[p * N_VARIANTS + variant])

            probs = []
            for s in scores:
                m = jnp.max(s, axis=-1, keepdims=True)
                e = jnp.exp(s - m)
                l = jnp.sum(e, axis=-1, keepdims=True)
                probs.append((e.astype(BF16), l, m))

            for (_, k0, _, rows), (e, l, m) in zip(blocks, probs):
                o = jnp.dot(e, vg[pl.ds(k0, KWIN), :], preferred_element_type=F32) / l
                lse = jnp.broadcast_to(m + jnp.log(l), (QBLK, LANES))
                if p == 0:
                    acc_s[rows, :] = o
                    lse_s[rows, :] = lse
                else:
                    o_old = acc_s[rows, :]
                    l_old = lse_s[rows, :]
                    mx = jnp.maximum(l_old, lse)
                    tot = mx + jnp.log(jnp.exp(l_old - mx) + jnp.exp(lse - mx))
                    acc_s[rows, :] = o_old * jnp.exp(l_old - tot) + o * jnp.exp(lse - tot)
                    lse_s[rows, :] = tot
            return carry

        lax.fori_loop(0, SEQ // QBLK // ATT_GROUP, group, 0)

    def finish(ch, carry):
        rows = pl.ds(pl.multiple_of(ch * ST_ROWS, ST_ROWS), ST_ROWS)
        g = g_ref[0, rows, :].astype(F32)
        o_ref[rows, :] = (acc_s[rows, :] * (g * _sigmoid(g))).astype(o_ref.dtype)
        return carry

    lax.fori_loop(0, SEQ // ST_ROWS, finish, 0)


def _attention(h_slabs, q_gain, k_gain, rel_bias):
    slab = lambda s0: pl.BlockSpec((1, SEQ, LANES), lambda h: (s0 + h, 0, 0))
    idx = jnp.asarray(_bucket_tiles())
    n_tiles = idx.shape[0]
    return pl.pallas_call(
        _attn_kernel,
        out_shape=jax.ShapeDtypeStruct((SEQ, HEADS * HEAD_DIM), BF16),
        grid=(HEADS,),
        in_specs=[pl.BlockSpec(memory_space=pltpu.SMEM),
                  slab(SLAB_Q), slab(SLAB_K), slab(SLAB_V), slab(SLAB_AGATE),
                  pl.BlockSpec((1, HEAD_DIM), lambda h: (0, 0)),
                  pl.BlockSpec((1, HEAD_DIM), lambda h: (0, 0)),
                  pl.BlockSpec((n_tiles, QBLK, KWIN), lambda h: (0, 0, 0))],
        out_specs=pl.BlockSpec((SEQ, LANES), lambda h: (0, h)),
        scratch_shapes=[pltpu.VMEM((SEQ, LANES), F32), pltpu.VMEM((SEQ, LANES), F32),
                        pltpu.VMEM((SEQ, LANES), F32), pltpu.VMEM((SEQ, LANES), F32),
                        pltpu.VMEM((SEQ, LANES), BF16), pltpu.VMEM((SEQ, LANES), BF16),
                        pltpu.VMEM((SEQ, LANES), BF16),
                        pltpu.VMEM((n_tiles, QBLK, KWIN), F32),
                        pltpu.VMEM((SEQ, LANES), F32), pltpu.VMEM((SEQ, LANES), F32)],
        compiler_params=pltpu.CompilerParams(
            dimension_semantics=("arbitrary",), vmem_limit_bytes=VMEM_LIMIT),
        name="dilated_attn",
    )(rel_bias, h_slabs, h_slabs, h_slabs, h_slabs,
      q_gain.reshape(1, HEAD_DIM), k_gain.reshape(1, HEAD_DIM), idx)


OP_BM = 1024
OP_BN = 512


def _out_proj_kernel(c_ref, a_ref, w_ref, x_ref, o_ref):
    acc = jnp.dot(c_ref[...], w_ref[0:CONV_CH, :], preferred_element_type=F32)
    acc = acc + jnp.dot(a_ref[...], w_ref[CONV_CH:, :], preferred_element_type=F32)
    o_ref[...] = x_ref[...] + acc


def _out_proj(conv_out, attn_out, w, x):
    return pl.pallas_call(
        _out_proj_kernel,
        out_shape=jax.ShapeDtypeStruct((SEQ, D_MODEL), F32),
        grid=(SEQ // OP_BM, D_MODEL // OP_BN),
        in_specs=[pl.BlockSpec((OP_BM, CONV_CH), lambda i, j: (i, 0)),
                  pl.BlockSpec((OP_BM, HEADS * HEAD_DIM), lambda i, j: (i, 0)),
                  pl.BlockSpec((D_MODEL, OP_BN), lambda i, j: (0, j)),
                  pl.BlockSpec((OP_BM, OP_BN), lambda i, j: (i, j))],
        out_specs=pl.BlockSpec((OP_BM, OP_BN), lambda i, j: (i, j)),
        compiler_params=pltpu.CompilerParams(
            dimension_semantics=("arbitrary", "arbitrary"), vmem_limit_bytes=VMEM_LIMIT),
        name="out_proj",
    )(conv_out, attn_out, w, x)


def kernel(x, norm_g, w_in, q_norm_g, k_norm_g, rel_bias, conv_w, conv_b, conv_ln_g,
           conv_ln_b, w_out):
    batch, seq, d_model = x.shape
    assert (batch, seq, d_model) == (1, SEQ, D_MODEL)
    depth = norm_g.shape[0]
    y = x.reshape(SEQ, D_MODEL)
    for l in range(depth):
        xn = _rmsnorm(y, norm_g[l].reshape(1, D_MODEL))
        h = _in_proj(xn, w_in[l].astype(BF16))
        conv_out = _conv_branch(h, conv_w[l], conv_b[l], conv_ln_g[l], conv_ln_b[l])
        attn_out = _attention(h, q_norm_g[l], k_norm_g[l], rel_bias)
        y = _out_proj(conv_out, attn_out, w_out[l].astype(BF16), y)
    return y.reshape(batch, seq, d_model)
```

```python
import math

import jax
import jax.numpy as jnp
import numpy as np
from jax import lax
from jax.experimental import pallas as pl
from jax.experimental.pallas import tpu as pltpu

F32 = jnp.float32
BF16 = jnp.bfloat16

SEQ = 8192
D_MODEL = 4096
CONV_CH = 2048
HEADS = 16
HEAD_DIM = 128
CONV_K = 31
PATTERNS = ((128, 1), (512, 4), (2048, 16))
RADIUS = 64
REL_BUCKETS = 32
REL_MAX_DISTANCE = 1024
NORM_EPS = 1e-6
LN_EPS = 1e-5
NEG_INF = -1e30
IN_WIDTH = 3 * CONV_CH + 4 * HEADS * HEAD_DIM

LANES = 128
CONV_SLABS = CONV_CH // LANES
SLAB_VAL, SLAB_GLU, SLAB_CGATE = 0, 16, 32
SLAB_Q, SLAB_K, SLAB_V, SLAB_AGATE = 0, 16, 32, 48

VMEM_LIMIT = 60 * 1024 * 1024

QBLK = 128
KWIN = QBLK + 2 * RADIUS


def _sigmoid(x):
    return 1.0 / (1.0 + jnp.exp(-x))


RN_ROWS = 512


IP_BM = 1024
IP_BN = 1024
IP_ROW_BLOCKS = SEQ // IP_BM
IP_SLABS = IP_BN // LANES
CONV_TILES = 3 * CONV_CH // IP_BN
ATTN_TILES = (IN_WIDTH - 3 * CONV_CH) // IP_BN
CV_T = SEQ // (ATTN_TILES * IP_ROW_BLOCKS)
CV_HALO = 16
W_ROWS = D_MODEL
W_PIECES = IP_ROW_BLOCKS // 2
W_PIECE = W_ROWS // W_PIECES


def _weight_scratch(bn):
    return [pltpu.VMEM((W_PIECE, bn), F32), pltpu.VMEM((2, W_ROWS, bn), BF16),
            pltpu.SemaphoreType.DMA(())]


def _stream_weights(w_hbm, wf_s, wb_s, sem, first_tile, bn):
    j = pl.program_id(0)
    i = pl.program_id(1)
    n_j = pl.num_programs(0)
    slot = j % 2

    def copy(tile, piece):
        col = pl.multiple_of((first_tile + tile) * bn, bn)
        return pltpu.make_async_copy(
            w_hbm.at[pl.ds(piece * W_PIECE, W_PIECE), pl.ds(col, bn)], wf_s, sem)

    def cast_into(dst_slot, piece):
        wb_s[dst_slot, pl.ds(piece * W_PIECE, W_PIECE), :] = wf_s[...].astype(BF16)

    @pl.when((j == 0) & (i == 0))
    def _first_tile():
        for piece in range(W_PIECES):
            c = copy(0, piece)
            c.start()
            c.wait()
            cast_into(0, piece)

    has_next = j + 1 < n_j
    for piece in range(W_PIECES):
        @pl.when(has_next & (i == 2 * piece))
        def _start(piece=piece):
            copy(j + 1, piece).start()

        @pl.when(has_next & (i == 2 * piece + 1))
        def _finish(piece=piece):
            copy(j + 1, piece).wait()
            cast_into(1 - slot, piece)

    return wb_s.at[slot]


def _matmul_to_slabs(x_ref, w_ref, o_ref):
    acc = jnp.dot(x_ref[...], w_ref[...], preferred_element_type=F32)
    for c in range(IP_SLABS):
        o_ref[c] = acc[:, c * LANES:(c + 1) * LANES].astype(o_ref.dtype)


NORM_TILES = 1


def _in_proj_kernel(x_ref, w_hbm, o_ref, wf_s, wb_s, sem):
    w_ref = _stream_weights(w_hbm, wf_s, wb_s, sem, NORM_TILES, IP_BN)
    _matmul_to_slabs(x_ref, w_ref, o_ref)


def _norm_proj_kernel(x_ref, g_ref, w_hbm, xn_ref, o_ref, wf_s, wb_s, sem):
    @pl.when(pl.program_id(0) == 0)
    def _load_weights():
        for piece in range(W_PIECES):
            rows = pl.ds(piece * W_PIECE, W_PIECE)
            c = pltpu.make_async_copy(w_hbm.at[rows, pl.ds(0, IP_BN)], wf_s, sem)
            c.start()
            c.wait()
            wb_s[rows, :] = wf_s[...].astype(BF16)

    x = x_ref[...]
    ms = jnp.mean(x * x, axis=-1, keepdims=True)
    xn = (x * lax.rsqrt(ms + NORM_EPS) * g_ref[...]).astype(BF16)
    xn_ref[...] = xn
    acc = jnp.dot(xn, wb_s[...], preferred_element_type=F32)
    for c in range(IP_SLABS):
        o_ref[c] = acc[:, c * LANES:(c + 1) * LANES].astype(o_ref.dtype)


def _rmsnorm_first_tile(x, g, w):
    return pl.pallas_call(
        _norm_proj_kernel,
        out_shape=(jax.ShapeDtypeStruct((SEQ, D_MODEL), BF16),
                   jax.ShapeDtypeStruct((IP_SLABS, SEQ, LANES), BF16)),
        grid=(SEQ // RN_ROWS,),
        in_specs=[pl.BlockSpec((RN_ROWS, D_MODEL), lambda i: (i, 0)),
                  pl.BlockSpec((1, D_MODEL), lambda i: (0, 0)),
                  pl.BlockSpec(memory_space=pl.ANY)],
        out_specs=(pl.BlockSpec((RN_ROWS, D_MODEL), lambda i: (i, 0)),
                   pl.BlockSpec((IP_SLABS, RN_ROWS, LANES), lambda i: (0, i, 0))),
        scratch_shapes=[pltpu.VMEM((W_PIECE, IP_BN), F32), pltpu.VMEM((W_ROWS, IP_BN), BF16),
                        pltpu.SemaphoreType.DMA(())],
        compiler_params=pltpu.CompilerParams(
            dimension_semantics=("arbitrary",), vmem_limit_bytes=VMEM_LIMIT),
        name="rmsnorm_first_tile",
    )(x, g, w)


def _slab(refs, idx):
    split = NORM_TILES * IP_SLABS
    return refs[0][idx] if idx < split else refs[1][idx - split]


def _conv_rows(h_refs, hp_refs, hn_refs, w_ref, b_ref, lng_ref, lnb_ref, o_ref, a_s, c_s,
               first, last):
    def glu(refs, c):
        return (_slab(refs, SLAB_VAL + c).astype(F32)
                * _sigmoid(_slab(refs, SLAB_GLU + c).astype(F32)))

    for c in range(CONV_SLABS):
        a_s[c, CV_HALO:CV_HALO + CV_T, :] = glu(h_refs, c)
        a_s[c, 0:CV_HALO, :] = jnp.where(first, 0.0, glu(hp_refs, c))
        a_s[c, CV_HALO + CV_T:, :] = jnp.where(last, 0.0, glu(hn_refs, c))

    s1 = jnp.zeros((CV_T, LANES), F32)
    for c in range(CONV_SLABS):
        acc = jnp.broadcast_to(b_ref[c], (CV_T, LANES))
        for j in range(CONV_K):
            off = j + CV_HALO - CONV_K // 2
            acc = acc + w_ref[c, j:j + 1, :] * a_s[c, off:off + CV_T, :]
        c_s[c] = acc
        s1 = s1 + acc
    mu = jnp.sum(s1, axis=-1, keepdims=True) * (1.0 / CONV_CH)

    s2 = jnp.zeros((CV_T, LANES), F32)
    for c in range(CONV_SLABS):
        d = c_s[c] - mu
        s2 = s2 + d * d
    rstd = lax.rsqrt(jnp.sum(s2, axis=-1, keepdims=True) * (1.0 / CONV_CH) + LN_EPS)

    for c in range(CONV_SLABS):
        y = (c_s[c] - mu) * rstd * lng_ref[c] + lnb_ref[c]
        y = y * _sigmoid(y)
        g = _slab(h_refs, SLAB_CGATE + c).astype(F32)
        y = y * (g * _sigmoid(g))
        o_ref[:, c * LANES:(c + 1) * LANES] = y.astype(o_ref.dtype)


def _in_proj_conv_kernel(x_ref, w_hbm, h0_ref, h1_ref, h0p_ref, h1p_ref, h0n_ref, h1n_ref,
                         cw_ref, cb_ref, lng_ref, lnb_ref,
                         o_ref, co_ref, wf_s, wb_s, sem, a_s, c_s):
    w_ref = _stream_weights(w_hbm, wf_s, wb_s, sem, CONV_TILES, IP_BN)
    _matmul_to_slabs(x_ref, w_ref, o_ref)
    rb = pl.program_id(0) * pl.num_programs(1) + pl.program_id(1)
    n_rb = pl.num_programs(0) * pl.num_programs(1)
    _conv_rows((h0_ref, h1_ref), (h0p_ref, h1p_ref), (h0n_ref, h1n_ref),
               cw_ref, cb_ref, lng_ref, lnb_ref, co_ref, a_s, c_s, rb == 0, rb == n_rb - 1)


def _in_proj_conv_cols(xn, w):
    return pl.pallas_call(
        _in_proj_kernel,
        out_shape=jax.ShapeDtypeStruct(((CONV_TILES - NORM_TILES) * IP_SLABS, SEQ, LANES), BF16),
        grid=(CONV_TILES - NORM_TILES, IP_ROW_BLOCKS),
        in_specs=[pl.BlockSpec((IP_BM, D_MODEL), lambda j, i: (i, 0)),
                  pl.BlockSpec(memory_space=pl.ANY)],
        out_specs=pl.BlockSpec((IP_SLABS, IP_BM, LANES), lambda j, i: (j, i, 0)),
        scratch_shapes=_weight_scratch(IP_BN),
        compiler_params=pltpu.CompilerParams(
            dimension_semantics=("arbitrary", "arbitrary"), vmem_limit_bytes=VMEM_LIMIT),
        name="in_proj_conv_cols",
    )(xn, w)


def _in_proj_attn_cols_and_conv(xn, w, c0, c1, conv_w, conv_b, ln_g, ln_b):
    hb = CV_T // CV_HALO
    last = SEQ // CV_HALO - 1
    rb = lambda j, i: j * IP_ROW_BLOCKS + i
    main = lambda a: pl.BlockSpec((a.shape[0], CV_T, LANES), lambda j, i: (0, rb(j, i), 0))
    prev = lambda a: pl.BlockSpec((a.shape[0], CV_HALO, LANES),
                                  lambda j, i: (0, jnp.maximum(rb(j, i) * hb - 1, 0), 0))
    nxt = lambda a: pl.BlockSpec((a.shape[0], CV_HALO, LANES),
                                 lambda j, i: (0, jnp.minimum((rb(j, i) + 1) * hb, last), 0))
    full3 = lambda a: pl.BlockSpec(a.shape, lambda j, i: (0, 0, 0))
    cw = conv_w.reshape(CONV_K, CONV_SLABS, LANES).transpose(1, 0, 2)
    cb = conv_b.reshape(CONV_SLABS, 1, LANES)
    g = ln_g.reshape(CONV_SLABS, 1, LANES)
    be = ln_b.reshape(CONV_SLABS, 1, LANES)
    return pl.pallas_call(
        _in_proj_conv_kernel,
        out_shape=(jax.ShapeDtypeStruct((ATTN_TILES * IP_SLABS, SEQ, LANES), BF16),
                   jax.ShapeDtypeStruct((SEQ, CONV_CH), BF16)),
        grid=(ATTN_TILES, IP_ROW_BLOCKS),
        in_specs=[pl.BlockSpec((IP_BM, D_MODEL), lambda j, i: (i, 0)),
                  pl.BlockSpec(memory_space=pl.ANY),
                  main(c0), main(c1), prev(c0), prev(c1), nxt(c0), nxt(c1),
                  full3(cw), full3(cb), full3(g), full3(be)],
        out_specs=(pl.BlockSpec((IP_SLABS, IP_BM, LANES), lambda j, i: (j, i, 0)),
                   pl.BlockSpec((CV_T, CONV_CH), lambda j, i: (rb(j, i), 0))),
        scratch_shapes=_weight_scratch(IP_BN) + [
            pltpu.VMEM((CONV_SLABS, CV_T + 2 * CV_HALO, LANES), F32),
            pltpu.VMEM((CONV_SLABS, CV_T, LANES), F32)],
        compiler_params=pltpu.CompilerParams(
            dimension_semantics=("arbitrary", "arbitrary"), vmem_limit_bytes=VMEM_LIMIT),
        name="in_proj_attn_cols_conv",
    )(xn, w, c0, c1, c0, c1, c0, c1, cw, cb, g, be)


N_VARIANTS = 3
ST_ROWS = 2048
GA_ROWS = 512
GA_UNROLL = 4
ATT_GROUP = 4
N_GROUPS = SEQ // QBLK // ATT_GROUP


def _t5_bucket_np(rel):
    half = REL_BUCKETS // 2
    exact = half // 2
    n = np.abs(rel)
    nf = np.maximum(n, 1).astype(np.float32)
    large = exact + (np.log(nf / np.float32(exact)) / np.float32(math.log(REL_MAX_DISTANCE / exact))
                     * np.float32(half - exact)).astype(np.int32)
    large = np.minimum(large, half - 1)
    return np.where(rel > 0, half, 0) + np.where(n < exact, n, large)


def _bucket_tile(dil, variant):
    rel = np.arange(KWIN)[None, :] - variant * RADIUS - np.arange(QBLK)[:, None]
    return np.where(np.abs(rel) <= RADIUS, _t5_bucket_np(rel * dil), -1).astype(np.int32)


def _interior_bucket_tiles():
    return np.stack([_bucket_tile(dil, 1) for _, dil in PATTERNS])


def _check_edge_variants():
    for _, dil in PATTERNS:
        mid = _bucket_tile(dil, 1)
        fill = np.full((QBLK, RADIUS), -1, np.int32)
        assert (_bucket_tile(dil, 0) == np.concatenate([mid[:, RADIUS:], fill], axis=1)).all()
        assert (_bucket_tile(dil, 2) == np.concatenate([fill, mid[:, :-RADIUS]], axis=1)).all()


_check_edge_variants()
_TILE_BUCKETS = tuple(tuple(int(b) for b in np.unique(t) if b >= 0)
                      for t in _interior_bucket_tiles())


def _gather4_body(src, dst_f, dst_b, seg):
    n_seg = SEQ // seg
    chunks_per_class = seg // 4 // GA_ROWS

    def body(i, carry):
        cls = i // chunks_per_class
        ch = i % chunks_per_class
        g = cls % n_seg
        s = cls // n_seg
        x = src[pl.ds(g * seg + ch * (4 * GA_ROWS) + s, GA_ROWS, stride=4), :]
        d0 = pl.multiple_of(i * GA_ROWS, GA_ROWS)
        if dst_f is not None:
            dst_f[pl.ds(d0, GA_ROWS), :] = x
        dst_b[pl.ds(d0, GA_ROWS), :] = x.astype(BF16)
        return carry

    return body


def _attn_kernel(rb_ref, q_ref, k_ref, v_ref, g_ref, gq_ref, gk_ref, idx_ref, o_ref,
                 f0, f1, f2, f3, qg, kg, vg, bias_s, acc_s, lse_s):
    hh = pl.program_id(0)
    scale = HEAD_DIM ** -0.5

    lane = lax.broadcasted_iota(jnp.int32, (QBLK, KWIN), 1)
    for p, buckets in enumerate(_TILE_BUCKETS):
        idx = idx_ref[p]
        tile = jnp.full((QBLK, KWIN), NEG_INF, F32)
        for b in buckets:
            tile = jnp.where(idx == b, rb_ref[b, hh], tile)
        bias_s[p * N_VARIANTS + 1] = tile
        bias_s[p * N_VARIANTS] = jnp.where(
            lane < KWIN - RADIUS, pltpu.roll(tile, KWIN - RADIUS, 1), NEG_INF)
        bias_s[p * N_VARIANTS + 2] = jnp.where(
            lane >= RADIUS, pltpu.roll(tile, RADIUS, 1), NEG_INF)

    def stage(ch, carry):
        rows = pl.ds(pl.multiple_of(ch * ST_ROWS, ST_ROWS), ST_ROWS)
        for src_ref, gain_ref, dst_f, dst_b in ((q_ref, gq_ref, f0, qg), (k_ref, gk_ref, f1, kg)):
            x = src_ref[0, rows, :].astype(F32)
            ms = jnp.mean(x * x, axis=-1, keepdims=True)
            x = x * lax.rsqrt(ms + NORM_EPS) * gain_ref[...]
            dst_f[rows, :] = x
            dst_b[rows, :] = x.astype(BF16)
        v = v_ref[0, rows, :]
        f2[rows, :] = v.astype(F32)
        vg[rows, :] = v
        return carry

    lax.fori_loop(0, SEQ // ST_ROWS, stage, 0, unroll=True)

    for p, (_, dil) in enumerate(PATTERNS):
        sub_len = SEQ // dil
        nb = sub_len // QBLK

        if p == 1:
            for src, dst_f, dst_b in ((f0, f3, qg), (f1, f0, kg), (f2, f1, vg)):
                lax.fori_loop(0, SEQ // GA_ROWS, _gather4_body(src, dst_f, dst_b, SEQ), 0,
                              unroll=GA_UNROLL)
        elif p == 2:
            for src, dst_b in ((f3, qg), (f0, kg), (f1, vg)):
                lax.fori_loop(0, SEQ // GA_ROWS, _gather4_body(src, None, dst_b, SEQ // 4), 0,
                              unroll=GA_UNROLL)

        def block_index(i, p=p, dil=dil, sub_len=sub_len, nb=nb):
            r, b = divmod(i, nb)
            q0 = i * QBLK
            variant = 0 if b == 0 else (2 if b == nb - 1 else 1)
            k0 = r * sub_len + b * QBLK - variant * RADIUS
            if p == 0:
                rows = pl.ds(q0, QBLK)
            else:
                rows = pl.ds(dil * QBLK * b + r, QBLK, stride=dil)
            return q0, k0, variant, rows

        def score_group(gi, p=p):
            out = []
            for u in range(ATT_GROUP):
                q0, k0, variant, _ = block_index(gi * ATT_GROUP + u)
                s = lax.dot_general(qg[q0:q0 + QBLK, :], kg[k0:k0 + KWIN, :],
                                    (((1,), (1,)), ((), ())), preferred_element_type=F32)
                out.append(s * scale + bias_s[p * N_VARIANTS + variant])
            return out

        def group(gi, scores, p=p):
            nxt = score_group(gi + 1) if gi + 1 < N_GROUPS else None
            blocks = [block_index(gi * ATT_GROUP + u) for u in range(ATT_GROUP)]

            probs = []
            for s in scores:
                m = jnp.max(s, axis=-1, keepdims=True)
                e = jnp.exp(s - m)
                l = jnp.sum(e, axis=-1, keepdims=True)
                probs.append((e.astype(BF16), l, m))

            for (_, k0, _, rows), (e, l, m) in zip(blocks, probs):
                o = jnp.dot(e, vg[k0:k0 + KWIN, :], preferred_element_type=F32) / l
                lse = jnp.broadcast_to(m + jnp.log(l), (QBLK, LANES))
                if p == 0:
                    acc_s[rows, :] = o
                    lse_s[rows, :] = lse
                else:
                    o_old = acc_s[rows, :]
                    l_old = lse_s[rows, :]
                    w_new = 1.0 / (1.0 + jnp.exp(l_old - lse))
                    w_old = 1.0 - w_new
                    acc_s[rows, :] = o_old * w_old + o * w_new
                    if p + 1 < len(PATTERNS):
                        lse_s[rows, :] = (jnp.maximum(l_old, lse)
                                          - jnp.log(jnp.maximum(w_new, w_old)))
            return nxt

        scores = score_group(0)
        for gi in range(N_GROUPS):
            scores = group(gi, scores)

    def finish(ch, carry):
        rows = pl.ds(pl.multiple_of(ch * ST_ROWS, ST_ROWS), ST_ROWS)
        g = g_ref[0, rows, :].astype(F32)
        o_ref[rows, :] = (acc_s[rows, :] * (g * _sigmoid(g))).astype(o_ref.dtype)
        return carry

    lax.fori_loop(0, SEQ // ST_ROWS, finish, 0, unroll=True)


def _attention(h_slabs, q_gain, k_gain, rel_bias):
    slab = lambda s0: pl.BlockSpec((1, SEQ, LANES), lambda h: (s0 + h, 0, 0))
    idx = jnp.asarray(_interior_bucket_tiles())
    n_tiles = len(PATTERNS) * N_VARIANTS
    return pl.pallas_call(
        _attn_kernel,
        out_shape=jax.ShapeDtypeStruct((SEQ, HEADS * HEAD_DIM), BF16),
        grid=(HEADS,),
        in_specs=[pl.BlockSpec(memory_space=pltpu.SMEM),
                  slab(SLAB_Q), slab(SLAB_K), slab(SLAB_V), slab(SLAB_AGATE),
                  pl.BlockSpec((1, HEAD_DIM), lambda h: (0, 0)),
                  pl.BlockSpec((1, HEAD_DIM), lambda h: (0, 0)),
                  pl.BlockSpec(idx.shape, lambda h: (0, 0, 0))],
        out_specs=pl.BlockSpec((SEQ, LANES), lambda h: (0, h)),
        scratch_shapes=[pltpu.VMEM((SEQ, LANES), F32), pltpu.VMEM((SEQ, LANES), F32),
                        pltpu.VMEM((SEQ, LANES), F32), pltpu.VMEM((SEQ, LANES), F32),
                        pltpu.VMEM((SEQ, LANES), BF16), pltpu.VMEM((SEQ, LANES), BF16),
                        pltpu.VMEM((SEQ, LANES), BF16),
                        pltpu.VMEM((n_tiles, QBLK, KWIN), F32),
                        pltpu.VMEM((SEQ, LANES), F32), pltpu.VMEM((SEQ, LANES), F32)],
        compiler_params=pltpu.CompilerParams(
            dimension_semantics=("arbitrary",), vmem_limit_bytes=VMEM_LIMIT),
        name="dilated_attn",
    )(rel_bias, h_slabs, h_slabs, h_slabs, h_slabs,
      q_gain.reshape(1, HEAD_DIM), k_gain.reshape(1, HEAD_DIM), idx)


OP_BM = 1024
OP_BN = 1024


def _out_proj_kernel(c_ref, a_ref, w_hbm, x_ref, o_ref, wf_s, wb_s, sem):
    w_ref = _stream_weights(w_hbm, wf_s, wb_s, sem, 0, OP_BN)
    lhs = jnp.concatenate([c_ref[...], a_ref[...]], axis=1)
    o_ref[...] = x_ref[...] + jnp.dot(lhs, w_ref[...], preferred_element_type=F32)


def _out_proj(conv_out, attn_out, w, x):
    assert SEQ // OP_BM == IP_ROW_BLOCKS
    return pl.pallas_call(
        _out_proj_kernel,
        out_shape=jax.ShapeDtypeStruct((SEQ, D_MODEL), F32),
        grid=(D_MODEL // OP_BN, SEQ // OP_BM),
        in_specs=[pl.BlockSpec((OP_BM, CONV_CH), lambda j, i: (i, 0)),
                  pl.BlockSpec((OP_BM, HEADS * HEAD_DIM), lambda j, i: (i, 0)),
                  pl.BlockSpec(memory_space=pl.ANY),
                  pl.BlockSpec((OP_BM, OP_BN), lambda j, i: (i, j))],
        out_specs=pl.BlockSpec((OP_BM, OP_BN), lambda j, i: (i, j)),
        scratch_shapes=_weight_scratch(OP_BN),
        compiler_params=pltpu.CompilerParams(
            dimension_semantics=("arbitrary", "arbitrary"), vmem_limit_bytes=VMEM_LIMIT),
        name="out_proj",
    )(conv_out, attn_out, w, x)


def kernel(x, norm_g, w_in, q_norm_g, k_norm_g, rel_bias, conv_w, conv_b, conv_ln_g,
           conv_ln_b, w_out):
    batch, seq, d_model = x.shape
    assert (batch, seq, d_model) == (1, SEQ, D_MODEL)
    depth = norm_g.shape[0]
    y = x.reshape(SEQ, D_MODEL)
    for l in range(depth):
        xn, hc0 = _rmsnorm_first_tile(y, norm_g[l].reshape(1, D_MODEL), w_in[l])
        hc1 = _in_proj_conv_cols(xn, w_in[l])
        ha, conv_out = _in_proj_attn_cols_and_conv(
            xn, w_in[l], hc0, hc1, conv_w[l], conv_b[l], conv_ln_g[l], conv_ln_b[l])
        attn_out = _attention(ha, q_norm_g[l], k_norm_g[l], rel_bias)
        y = _out_proj(conv_out, attn_out, w_out[l], y)
    return y.reshape(batch, seq, d_model)
```

```python
import math

import jax
import jax.numpy as jnp
import numpy as np
from jax import lax
from jax.experimental import pallas as pl
from jax.experimental.pallas import tpu as pltpu

F32 = jnp.float32
BF16 = jnp.bfloat16

SEQ = 8192
D_MODEL = 4096
CONV_CH = 2048
HEADS = 16
HEAD_DIM = 128
CONV_K = 31
PATTERNS = ((128, 1), (512, 4), (2048, 16))
RADIUS = 64
REL_BUCKETS = 32
REL_MAX_DISTANCE = 1024
NORM_EPS = 1e-6
LN_EPS = 1e-5
NEG_INF = -1e30
IN_WIDTH = 3 * CONV_CH + 4 * HEADS * HEAD_DIM

LANES = 128
CONV_SLABS = CONV_CH // LANES
SLAB_VAL, SLAB_GLU, SLAB_CGATE = 0, 16, 32
SLAB_Q, SLAB_K, SLAB_V, SLAB_AGATE = 0, 16, 32, 48

VMEM_LIMIT = 60 * 1024 * 1024

QBLK = 128
KWIN = QBLK + 2 * RADIUS


NEG_LOG2E = -math.log2(math.e)


def _sigmoid(x):
    return 1.0 / (1.0 + jnp.exp2(x * NEG_LOG2E))


RN_ROWS = 512


IP_BM = 1024
IP_BN = 1024
IP_ROW_BLOCKS = SEQ // IP_BM
IP_SLABS = IP_BN // LANES
CONV_TILES = 3 * CONV_CH // IP_BN
ATTN_TILES = (IN_WIDTH - 3 * CONV_CH) // IP_BN
CV_T = SEQ // (ATTN_TILES * IP_ROW_BLOCKS)
CV_HALO = 16
W_ROWS = D_MODEL
W_PIECES = IP_ROW_BLOCKS // 2
W_PIECE = W_ROWS // W_PIECES


def _weight_scratch(bn):
    return [pltpu.VMEM((W_PIECE, bn), F32), pltpu.VMEM((2, W_ROWS, bn), BF16),
            pltpu.SemaphoreType.DMA(())]


def _stream_weights(w_hbm, wf_s, wb_s, sem, first_tile, bn):
    j = pl.program_id(0)
    i = pl.program_id(1)
    n_j = pl.num_programs(0)
    slot = j % 2

    def copy(tile, piece):
        col = pl.multiple_of((first_tile + tile) * bn, bn)
        return pltpu.make_async_copy(
            w_hbm.at[pl.ds(piece * W_PIECE, W_PIECE), pl.ds(col, bn)], wf_s, sem)

    def cast_into(dst_slot, piece):
        wb_s[dst_slot, pl.ds(piece * W_PIECE, W_PIECE), :] = wf_s[...].astype(BF16)

    @pl.when((j == 0) & (i == 0))
    def _first_tile():
        for piece in range(W_PIECES):
            c = copy(0, piece)
            c.start()
            c.wait()
            cast_into(0, piece)

    has_next = j + 1 < n_j
    for piece in range(W_PIECES):
        @pl.when(has_next & (i == 2 * piece))
        def _start(piece=piece):
            copy(j + 1, piece).start()

        @pl.when(has_next & (i == 2 * piece + 1))
        def _finish(piece=piece):
            copy(j + 1, piece).wait()
            cast_into(1 - slot, piece)

    return wb_s.at[slot]


def _matmul_to_slabs(x_ref, w_ref, o_ref):
    acc = jnp.dot(x_ref[...], w_ref[...], preferred_element_type=F32)
    for c in range(IP_SLABS):
        o_ref[c] = acc[:, c * LANES:(c + 1) * LANES].astype(o_ref.dtype)


NORM_TILES = 1


def _in_proj_kernel(x_ref, w_hbm, o_ref, wf_s, wb_s, sem):
    w_ref = _stream_weights(w_hbm, wf_s, wb_s, sem, NORM_TILES, IP_BN)
    _matmul_to_slabs(x_ref, w_ref, o_ref)


def _norm_proj_kernel(x_ref, g_ref, w_hbm, xn_ref, o_ref, wf_s, wb_s, sem):
    @pl.when(pl.program_id(0) == 0)
    def _load_weights():
        for piece in range(W_PIECES):
            rows = pl.ds(piece * W_PIECE, W_PIECE)
            c = pltpu.make_async_copy(w_hbm.at[rows, pl.ds(0, IP_BN)], wf_s, sem)
            c.start()
            c.wait()
            wb_s[rows, :] = wf_s[...].astype(BF16)

    x = x_ref[...]
    ms = jnp.mean(x * x, axis=-1, keepdims=True)
    xn = (x * lax.rsqrt(ms + NORM_EPS) * g_ref[...]).astype(BF16)
    xn_ref[...] = xn
    acc = jnp.dot(xn, wb_s[...], preferred_element_type=F32)
    for c in range(IP_SLABS):
        o_ref[c] = acc[:, c * LANES:(c + 1) * LANES].astype(o_ref.dtype)


def _rmsnorm_first_tile(x, g, w):
    return pl.pallas_call(
        _norm_proj_kernel,
        out_shape=(jax.ShapeDtypeStruct((SEQ, D_MODEL), BF16),
                   jax.ShapeDtypeStruct((IP_SLABS, SEQ, LANES), BF16)),
        grid=(SEQ // RN_ROWS,),
        in_specs=[pl.BlockSpec((RN_ROWS, D_MODEL), lambda i: (i, 0)),
                  pl.BlockSpec((1, D_MODEL), lambda i: (0, 0)),
                  pl.BlockSpec(memory_space=pl.ANY)],
        out_specs=(pl.BlockSpec((RN_ROWS, D_MODEL), lambda i: (i, 0)),
                   pl.BlockSpec((IP_SLABS, RN_ROWS, LANES), lambda i: (0, i, 0))),
        scratch_shapes=[pltpu.VMEM((W_PIECE, IP_BN), F32), pltpu.VMEM((W_ROWS, IP_BN), BF16),
                        pltpu.SemaphoreType.DMA(())],
        compiler_params=pltpu.CompilerParams(
            dimension_semantics=("arbitrary",), vmem_limit_bytes=VMEM_LIMIT),
        name="rmsnorm_first_tile",
    )(x, g, w)


def _slab(refs, idx):
    split = NORM_TILES * IP_SLABS
    return refs[0][idx] if idx < split else refs[1][idx - split]


def _conv_rows(h_refs, hp_refs, hn_refs, w_ref, b_ref, lng_ref, lnb_ref, o_ref, a_s, c_s,
               first, last):
    def glu(refs, c):
        return (_slab(refs, SLAB_VAL + c).astype(F32)
                * _sigmoid(_slab(refs, SLAB_GLU + c).astype(F32)))

    for c in range(CONV_SLABS):
        a_s[c, CV_HALO:CV_HALO + CV_T, :] = glu(h_refs, c)
        a_s[c, 0:CV_HALO, :] = jnp.where(first, 0.0, glu(hp_refs, c))
        a_s[c, CV_HALO + CV_T:, :] = jnp.where(last, 0.0, glu(hn_refs, c))

    s1 = jnp.zeros((CV_T, LANES), F32)
    for c in range(CONV_SLABS):
        acc = jnp.broadcast_to(b_ref[c], (CV_T, LANES))
        for j in range(CONV_K):
            off = j + CV_HALO - CONV_K // 2
            acc = acc + w_ref[c, j:j + 1, :] * a_s[c, off:off + CV_T, :]
        c_s[c] = acc
        s1 = s1 + acc
    mu = jnp.sum(s1, axis=-1, keepdims=True) * (1.0 / CONV_CH)

    s2 = jnp.zeros((CV_T, LANES), F32)
    for c in range(CONV_SLABS):
        d = c_s[c] - mu
        s2 = s2 + d * d
    rstd = lax.rsqrt(jnp.sum(s2, axis=-1, keepdims=True) * (1.0 / CONV_CH) + LN_EPS)

    for c in range(CONV_SLABS):
        y = (c_s[c] - mu) * rstd * lng_ref[c] + lnb_ref[c]
        y = y * _sigmoid(y)
        g = _slab(h_refs, SLAB_CGATE + c).astype(F32)
        y = y * (g * _sigmoid(g))
        o_ref[:, c * LANES:(c + 1) * LANES] = y.astype(o_ref.dtype)


def _in_proj_conv_kernel(x_ref, w_hbm, h0_ref, h1_ref, h0p_ref, h1p_ref, h0n_ref, h1n_ref,
                         cw_ref, cb_ref, lng_ref, lnb_ref,
                         o_ref, co_ref, wf_s, wb_s, sem, a_s, c_s):
    w_ref = _stream_weights(w_hbm, wf_s, wb_s, sem, CONV_TILES, IP_BN)
    _matmul_to_slabs(x_ref, w_ref, o_ref)
    rb = pl.program_id(0) * pl.num_programs(1) + pl.program_id(1)
    n_rb = pl.num_programs(0) * pl.num_programs(1)
    _conv_rows((h0_ref, h1_ref), (h0p_ref, h1p_ref), (h0n_ref, h1n_ref),
               cw_ref, cb_ref, lng_ref, lnb_ref, co_ref, a_s, c_s, rb == 0, rb == n_rb - 1)


def _in_proj_conv_cols(xn, w):
    return pl.pallas_call(
        _in_proj_kernel,
        out_shape=jax.ShapeDtypeStruct(((CONV_TILES - NORM_TILES) * IP_SLABS, SEQ, LANES), BF16),
        grid=(CONV_TILES - NORM_TILES, IP_ROW_BLOCKS),
        in_specs=[pl.BlockSpec((IP_BM, D_MODEL), lambda j, i: (i, 0)),
                  pl.BlockSpec(memory_space=pl.ANY)],
        out_specs=pl.BlockSpec((IP_SLABS, IP_BM, LANES), lambda j, i: (j, i, 0)),
        scratch_shapes=_weight_scratch(IP_BN),
        compiler_params=pltpu.CompilerParams(
            dimension_semantics=("arbitrary", "arbitrary"), vmem_limit_bytes=VMEM_LIMIT),
        name="in_proj_conv_cols",
    )(xn, w)


def _in_proj_attn_cols_and_conv(xn, w, c0, c1, conv_w, conv_b, ln_g, ln_b):
    hb = CV_T // CV_HALO
    last = SEQ // CV_HALO - 1
    rb = lambda j, i: j * IP_ROW_BLOCKS + i
    main = lambda a: pl.BlockSpec((a.shape[0], CV_T, LANES), lambda j, i: (0, rb(j, i), 0))
    prev = lambda a: pl.BlockSpec((a.shape[0], CV_HALO, LANES),
                                  lambda j, i: (0, jnp.maximum(rb(j, i) * hb - 1, 0), 0))
    nxt = lambda a: pl.BlockSpec((a.shape[0], CV_HALO, LANES),
                                 lambda j, i: (0, jnp.minimum((rb(j, i) + 1) * hb, last), 0))
    full3 = lambda a: pl.BlockSpec(a.shape, lambda j, i: (0, 0, 0))
    cw = conv_w.reshape(CONV_K, CONV_SLABS, LANES).transpose(1, 0, 2)
    cb = conv_b.reshape(CONV_SLABS, 1, LANES)
    g = ln_g.reshape(CONV_SLABS, 1, LANES)
    be = ln_b.reshape(CONV_SLABS, 1, LANES)
    return pl.pallas_call(
        _in_proj_conv_kernel,
        out_shape=(jax.ShapeDtypeStruct((ATTN_TILES * IP_SLABS, SEQ, LANES), BF16),
                   jax.ShapeDtypeStruct((SEQ, CONV_CH), BF16)),
        grid=(ATTN_TILES, IP_ROW_BLOCKS),
        in_specs=[pl.BlockSpec((IP_BM, D_MODEL), lambda j, i: (i, 0)),
                  pl.BlockSpec(memory_space=pl.ANY),
                  main(c0), main(c1), prev(c0), prev(c1), nxt(c0), nxt(c1),
                  full3(cw), full3(cb), full3(g), full3(be)],
        out_specs=(pl.BlockSpec((IP_SLABS, IP_BM, LANES), lambda j, i: (j, i, 0)),
                   pl.BlockSpec((CV_T, CONV_CH), lambda j, i: (rb(j, i), 0))),
        scratch_shapes=_weight_scratch(IP_BN) + [
            pltpu.VMEM((CONV_SLABS, CV_T + 2 * CV_HALO, LANES), F32),
            pltpu.VMEM((CONV_SLABS, CV_T, LANES), F32)],
        compiler_params=pltpu.CompilerParams(
            dimension_semantics=("arbitrary", "arbitrary"), vmem_limit_bytes=VMEM_LIMIT),
        name="in_proj_attn_cols_conv",
    )(xn, w, c0, c1, c0, c1, c0, c1, cw, cb, g, be)


N_VARIANTS = 3
ST_ROWS = 2048
GA_ROWS = 512
GA_UNROLL = 4
ATT_GROUP = 4
N_GROUPS = SEQ // QBLK // ATT_GROUP


def _t5_bucket_np(rel):
    half = REL_BUCKETS // 2
    exact = half // 2
    n = np.abs(rel)
    nf = np.maximum(n, 1).astype(np.float32)
    large = exact + (np.log(nf / np.float32(exact)) / np.float32(math.log(REL_MAX_DISTANCE / exact))
                     * np.float32(half - exact)).astype(np.int32)
    large = np.minimum(large, half - 1)
    return np.where(rel > 0, half, 0) + np.where(n < exact, n, large)


def _bucket_tile(dil, variant):
    rel = np.arange(KWIN)[None, :] - variant * RADIUS - np.arange(QBLK)[:, None]
    return np.where(np.abs(rel) <= RADIUS, _t5_bucket_np(rel * dil), -1).astype(np.int32)


def _interior_bucket_tiles():
    return np.stack([_bucket_tile(dil, 1) for _, dil in PATTERNS])


def _check_edge_variants():
    for _, dil in PATTERNS:
        mid = _bucket_tile(dil, 1)
        fill = np.full((QBLK, RADIUS), -1, np.int32)
        assert (_bucket_tile(dil, 0) == np.concatenate([mid[:, RADIUS:], fill], axis=1)).all()
        assert (_bucket_tile(dil, 2) == np.concatenate([fill, mid[:, :-RADIUS]], axis=1)).all()


_check_edge_variants()
_TILE_BUCKETS = tuple(tuple(int(b) for b in np.unique(t) if b >= 0)
                      for t in _interior_bucket_tiles())


def _gather4_body(src, dst_f, dst_b, seg):
    n_seg = SEQ // seg
    chunks_per_class = seg // 4 // GA_ROWS

    def body(i, carry):
        cls = i // chunks_per_class
        ch = i % chunks_per_class
        g = cls % n_seg
        s = cls // n_seg
        x = src[pl.ds(g * seg + ch * (4 * GA_ROWS) + s, GA_ROWS, stride=4), :]
        d0 = pl.multiple_of(i * GA_ROWS, GA_ROWS)
        if dst_f is not None:
            dst_f[pl.ds(d0, GA_ROWS), :] = x
        dst_b[pl.ds(d0, GA_ROWS), :] = x.astype(BF16)
        return carry

    return body


def _attn_kernel(rb_ref, q_ref, k_ref, v_ref, g_ref, gq_ref, gk_ref, idx_ref, o_ref,
                 f0, f1, f2, f3, qg, kg, vg, bias_s, acc_s, lse_s):
    hh = pl.program_id(0)
    scale = HEAD_DIM ** -0.5

    lane = lax.broadcasted_iota(jnp.int32, (QBLK, KWIN), 1)
    for p, buckets in enumerate(_TILE_BUCKETS):
        idx = idx_ref[p]
        tile = jnp.full((QBLK, KWIN), NEG_INF, F32)
        for b in buckets:
            tile = jnp.where(idx == b, rb_ref[b, hh], tile)
        bias_s[p * N_VARIANTS + 1] = tile
        bias_s[p * N_VARIANTS] = jnp.where(
            lane < KWIN - RADIUS, pltpu.roll(tile, KWIN - RADIUS, 1), NEG_INF)
        bias_s[p * N_VARIANTS + 2] = jnp.where(
            lane >= RADIUS, pltpu.roll(tile, RADIUS, 1), NEG_INF)

    def stage(ch, carry):
        rows = pl.ds(pl.multiple_of(ch * ST_ROWS, ST_ROWS), ST_ROWS)
        for src_ref, gain_ref, dst_f, dst_b in ((q_ref, gq_ref, f0, qg), (k_ref, gk_ref, f1, kg)):
            x = src_ref[0, rows, :].astype(F32)
            ms = jnp.mean(x * x, axis=-1, keepdims=True)
            x = x * lax.rsqrt(ms + NORM_EPS) * gain_ref[...]
            dst_f[rows, :] = x
            dst_b[rows, :] = x.astype(BF16)
        v = v_ref[0, rows, :]
        f2[rows, :] = v.astype(F32)
        vg[rows, :] = v
        return carry

    lax.fori_loop(0, SEQ // ST_ROWS, stage, 0, unroll=True)

    for p, (_, dil) in enumerate(PATTERNS):
        sub_len = SEQ // dil
        nb = sub_len // QBLK

        if p == 1:
            for src, dst_f, dst_b in ((f0, f3, qg), (f1, f0, kg), (f2, f1, vg)):
                lax.fori_loop(0, SEQ // GA_ROWS, _gather4_body(src, dst_f, dst_b, SEQ), 0,
                              unroll=GA_UNROLL)
        elif p == 2:
            for src, dst_b in ((f3, qg), (f0, kg), (f1, vg)):
                lax.fori_loop(0, SEQ // GA_ROWS, _gather4_body(src, None, dst_b, SEQ // 4), 0,
                              unroll=GA_UNROLL)

        def block_index(i, p=p, dil=dil, sub_len=sub_len, nb=nb):
            r, b = divmod(i, nb)
            q0 = i * QBLK
            variant = 0 if b == 0 else (2 if b == nb - 1 else 1)
            k0 = r * sub_len + b * QBLK - variant * RADIUS
            if p == 0:
                rows = pl.ds(q0, QBLK)
            else:
                rows = pl.ds(dil * QBLK * b + r, QBLK, stride=dil)
            return q0, k0, variant, rows

        def score_group(gi, p=p):
            out = []
            for u in range(ATT_GROUP):
                q0, k0, variant, _ = block_index(gi * ATT_GROUP + u)
                s = lax.dot_general(qg[q0:q0 + QBLK, :], kg[k0:k0 + KWIN, :],
                                    (((1,), (1,)), ((), ())), preferred_element_type=F32)
                out.append(s * scale + bias_s[p * N_VARIANTS + variant])
            return out

        def group(gi, scores, p=p):
            nxt = score_group(gi + 1) if gi + 1 < N_GROUPS else None
            blocks = [block_index(gi * ATT_GROUP + u) for u in range(ATT_GROUP)]

            probs = []
            for s in scores:
                m = jnp.max(s, axis=-1, keepdims=True)
                e = jnp.exp(s - m)
                l = jnp.sum(e, axis=-1, keepdims=True)
                probs.append((e.astype(BF16), l, m))

            for (_, k0, _, rows), (e, l, m) in zip(blocks, probs):
                o = jnp.dot(e, vg[k0:k0 + KWIN, :], preferred_element_type=F32) / l
                lse = jnp.broadcast_to(m + jnp.log(l), (QBLK, LANES))
                if p == 0:
                    acc_s[rows, :] = o
                    lse_s[rows, :] = lse
                else:
                    o_old = acc_s[rows, :]
                    l_old = lse_s[rows, :]
                    w_new = 1.0 / (1.0 + jnp.exp(l_old - lse))
                    w_old = 1.0 - w_new
                    acc_s[rows, :] = o_old * w_old + o * w_new
                    if p + 1 < len(PATTERNS):
                        lse_s[rows, :] = (jnp.maximum(l_old, lse)
                                          - jnp.log(jnp.maximum(w_new, w_old)))
            return nxt

        scores = score_group(0)
        for gi in range(N_GROUPS):
            scores = group(gi, scores)

    def finish(ch, carry):
        rows = pl.ds(pl.multiple_of(ch * ST_ROWS, ST_ROWS), ST_ROWS)
        g = g_ref[0, rows, :].astype(F32)
        o_ref[rows, :] = (acc_s[rows, :] * (g * _sigmoid(g))).astype(o_ref.dtype)
        return carry

    lax.fori_loop(0, SEQ // ST_ROWS, finish, 0, unroll=True)


def _attention(h_slabs, q_gain, k_gain, rel_bias):
    slab = lambda s0: pl.BlockSpec((1, SEQ, LANES), lambda h: (s0 + h, 0, 0))
    idx = jnp.asarray(_interior_bucket_tiles())
    n_tiles = len(PATTERNS) * N_VARIANTS
    return pl.pallas_call(
        _attn_kernel,
        out_shape=jax.ShapeDtypeStruct((SEQ, HEADS * HEAD_DIM), BF16),
        grid=(HEADS,),
        in_specs=[pl.BlockSpec(memory_space=pltpu.SMEM),
                  slab(SLAB_Q), slab(SLAB_K), slab(SLAB_V), slab(SLAB_AGATE),
                  pl.BlockSpec((1, HEAD_DIM), lambda h: (0, 0)),
                  pl.BlockSpec((1, HEAD_DIM), lambda h: (0, 0)),
                  pl.BlockSpec(idx.shape, lambda h: (0, 0, 0))],
        out_specs=pl.BlockSpec((SEQ, LANES), lambda h: (0, h)),
        scratch_shapes=[pltpu.VMEM((SEQ, LANES), F32), pltpu.VMEM((SEQ, LANES), F32),
                        pltpu.VMEM((SEQ, LANES), F32), pltpu.VMEM((SEQ, LANES), F32),
                        pltpu.VMEM((SEQ, LANES), BF16), pltpu.VMEM((SEQ, LANES), BF16),
                        pltpu.VMEM((SEQ, LANES), BF16),
                        pltpu.VMEM((n_tiles, QBLK, KWIN), F32),
                        pltpu.VMEM((SEQ, LANES), F32), pltpu.VMEM((SEQ, LANES), F32)],
        compiler_params=pltpu.CompilerParams(
            dimension_semantics=("arbitrary",), vmem_limit_bytes=VMEM_LIMIT),
        name="dilated_attn",
    )(rel_bias, h_slabs, h_slabs, h_slabs, h_slabs,
      q_gain.reshape(1, HEAD_DIM), k_gain.reshape(1, HEAD_DIM), idx)


OP_BM = 1024
OP_BN = 1024


def _out_proj_kernel(c_ref, a_ref, w_hbm, x_ref, o_ref, wf_s, wb_s, sem):
    w_ref = _stream_weights(w_hbm, wf_s, wb_s, sem, 0, OP_BN)
    lhs = jnp.concatenate([c_ref[...], a_ref[...]], axis=1)
    o_ref[...] = x_ref[...] + jnp.dot(lhs, w_ref[...], preferred_element_type=F32)


def _out_proj(conv_out, attn_out, w, x):
    assert SEQ // OP_BM == IP_ROW_BLOCKS
    return pl.pallas_call(
        _out_proj_kernel,
        out_shape=jax.ShapeDtypeStruct((SEQ, D_MODEL), F32),
        grid=(D_MODEL // OP_BN, SEQ // OP_BM),
        in_specs=[pl.BlockSpec((OP_BM, CONV_CH), lambda j, i: (i, 0)),
                  pl.BlockSpec((OP_BM, HEADS * HEAD_DIM), lambda j, i: (i, 0)),
                  pl.BlockSpec(memory_space=pl.ANY),
                  pl.BlockSpec((OP_BM, OP_BN), lambda j, i: (i, j))],
        out_specs=pl.BlockSpec((OP_BM, OP_BN), lambda j, i: (i, j)),
        scratch_shapes=_weight_scratch(OP_BN),
        compiler_params=pltpu.CompilerParams(
            dimension_semantics=("arbitrary", "arbitrary"), vmem_limit_bytes=VMEM_LIMIT),
        name="out_proj",
    )(conv_out, attn_out, w, x)


def kernel(x, norm_g, w_in, q_norm_g, k_norm_g, rel_bias, conv_w, conv_b, conv_ln_g,
           conv_ln_b, w_out):
    batch, seq, d_model = x.shape
    assert (batch, seq, d_model) == (1, SEQ, D_MODEL)
    depth = norm_g.shape[0]
    y = x.reshape(SEQ, D_MODEL)
    for l in range(depth):
        xn, hc0 = _rmsnorm_first_tile(y, norm_g[l].reshape(1, D_MODEL), w_in[l])
        hc1 = _in_proj_conv_cols(xn, w_in[l])
        ha, conv_out = _in_proj_attn_cols_and_conv(
            xn, w_in[l], hc0, hc1, conv_w[l], conv_b[l], conv_ln_g[l], conv_ln_b[l])
        attn_out = _attention(ha, q_norm_g[l], k_norm_g[l], rel_bias)
        y = _out_proj(conv_out, attn_out, w_out[l], y)
    return y.reshape(batch, seq, d_model)
```

```python
import math

import jax
import jax.numpy as jnp
import numpy as np
from jax import lax
from jax.experimental import pallas as pl
from jax.experimental.pallas import tpu as pltpu

F32 = jnp.float32
BF16 = jnp.bfloat16

SEQ = 8192
D_MODEL = 4096
CONV_CH = 2048
HEADS = 16
HEAD_DIM = 128
CONV_K = 31
PATTERNS = ((128, 1), (512, 4), (2048, 16))
RADIUS = 64
REL_BUCKETS = 32
REL_MAX_DISTANCE = 1024
NORM_EPS = 1e-6
LN_EPS = 1e-5
NEG_INF = -1e30
IN_WIDTH = 3 * CONV_CH + 4 * HEADS * HEAD_DIM

LANES = 128
CONV_SLABS = CONV_CH // LANES
SLAB_VAL, SLAB_GLU, SLAB_CGATE = 0, 16, 32
SLAB_Q, SLAB_K, SLAB_V, SLAB_AGATE = 0, 16, 32, 48

VMEM_LIMIT = 60 * 1024 * 1024

QBLK = 128
KWIN = QBLK + 2 * RADIUS


LOG2E = math.log2(math.e)
NEG_LOG2E = -LOG2E
LN2 = math.log(2.0)


def _sigmoid(x):
    return 1.0 / (1.0 + jnp.exp2(x * NEG_LOG2E))


RN_ROWS = 512


IP_BM = 1024
IP_BN = 1024
IP_ROW_BLOCKS = SEQ // IP_BM
IP_SLABS = IP_BN // LANES
CONV_TILES = 3 * CONV_CH // IP_BN
ATTN_TILES = (IN_WIDTH - 3 * CONV_CH) // IP_BN
CV_T = SEQ // (ATTN_TILES * IP_ROW_BLOCKS)
CV_HALO = 16
W_ROWS = D_MODEL
W_PIECES = IP_ROW_BLOCKS // 2
W_PIECE = W_ROWS // W_PIECES


def _weight_scratch(bn):
    return [pltpu.VMEM((W_PIECE, bn), F32), pltpu.VMEM((2, W_ROWS, bn), BF16),
            pltpu.SemaphoreType.DMA(())]


def _stream_weights(w_hbm, wf_s, wb_s, sem, first_tile, bn):
    j = pl.program_id(0)
    i = pl.program_id(1)
    n_j = pl.num_programs(0)
    slot = j % 2

    def copy(tile, piece):
        col = pl.multiple_of((first_tile + tile) * bn, bn)
        return pltpu.make_async_copy(
            w_hbm.at[pl.ds(piece * W_PIECE, W_PIECE), pl.ds(col, bn)], wf_s, sem)

    def cast_into(dst_slot, piece):
        wb_s[dst_slot, pl.ds(piece * W_PIECE, W_PIECE), :] = wf_s[...].astype(BF16)

    @pl.when((j == 0) & (i == 0))
    def _first_tile():
        for piece in range(W_PIECES):
            c = copy(0, piece)
            c.start()
            c.wait()
            cast_into(0, piece)

    has_next = j + 1 < n_j
    for piece in range(W_PIECES):
        @pl.when(has_next & (i == 2 * piece))
        def _start(piece=piece):
            copy(j + 1, piece).start()

        @pl.when(has_next & (i == 2 * piece + 1))
        def _finish(piece=piece):
            copy(j + 1, piece).wait()
            cast_into(1 - slot, piece)

    return wb_s.at[slot]


def _matmul_to_slabs(x_ref, w_ref, o_ref):
    acc = jnp.dot(x_ref[...], w_ref[...], preferred_element_type=F32)
    for c in range(IP_SLABS):
        o_ref[c] = acc[:, c * LANES:(c + 1) * LANES].astype(o_ref.dtype)


NORM_TILES = 1


def _in_proj_kernel(x_ref, w_hbm, o_ref, wf_s, wb_s, sem):
    w_ref = _stream_weights(w_hbm, wf_s, wb_s, sem, NORM_TILES, IP_BN)
    _matmul_to_slabs(x_ref, w_ref, o_ref)


def _norm_proj_kernel(x_ref, g_ref, w_hbm, xn_ref, o_ref, wf_s, wb_s, sem):
    @pl.when(pl.program_id(0) == 0)
    def _load_weights():
        for piece in range(W_PIECES):
            rows = pl.ds(piece * W_PIECE, W_PIECE)
            c = pltpu.make_async_copy(w_hbm.at[rows, pl.ds(0, IP_BN)], wf_s, sem)
            c.start()
            c.wait()
            wb_s[rows, :] = wf_s[...].astype(BF16)

    x = x_ref[...]
    ms = jnp.mean(x * x, axis=-1, keepdims=True)
    xn = (x * lax.rsqrt(ms + NORM_EPS) * g_ref[...]).astype(BF16)
    xn_ref[...] = xn
    acc = jnp.dot(xn, wb_s[...], preferred_element_type=F32)
    for c in range(IP_SLABS):
        o_ref[c] = acc[:, c * LANES:(c + 1) * LANES].astype(o_ref.dtype)


def _rmsnorm_first_tile(x, g, w):
    return pl.pallas_call(
        _norm_proj_kernel,
        out_shape=(jax.ShapeDtypeStruct((SEQ, D_MODEL), BF16),
                   jax.ShapeDtypeStruct((IP_SLABS, SEQ, LANES), BF16)),
        grid=(SEQ // RN_ROWS,),
        in_specs=[pl.BlockSpec((RN_ROWS, D_MODEL), lambda i: (i, 0)),
                  pl.BlockSpec((1, D_MODEL), lambda i: (0, 0)),
                  pl.BlockSpec(memory_space=pl.ANY)],
        out_specs=(pl.BlockSpec((RN_ROWS, D_MODEL), lambda i: (i, 0)),
                   pl.BlockSpec((IP_SLABS, RN_ROWS, LANES), lambda i: (0, i, 0))),
        scratch_shapes=[pltpu.VMEM((W_PIECE, IP_BN), F32), pltpu.VMEM((W_ROWS, IP_BN), BF16),
                        pltpu.SemaphoreType.DMA(())],
        compiler_params=pltpu.CompilerParams(
            dimension_semantics=("arbitrary",), vmem_limit_bytes=VMEM_LIMIT),
        name="rmsnorm_first_tile",
    )(x, g, w)


def _slab(refs, idx):
    split = NORM_TILES * IP_SLABS
    return refs[0][idx] if idx < split else refs[1][idx - split]


def _conv_rows(h_refs, hp_refs, hn_refs, w_ref, b_ref, lng_ref, lnb_ref, o_ref, a_s, c_s,
               first, last):
    def glu(refs, c):
        return (_slab(refs, SLAB_VAL + c).astype(F32)
                * _sigmoid(_slab(refs, SLAB_GLU + c).astype(F32)))

    for c in range(CONV_SLABS):
        a_s[c, CV_HALO:CV_HALO + CV_T, :] = glu(h_refs, c)
        a_s[c, 0:CV_HALO, :] = jnp.where(first, 0.0, glu(hp_refs, c))
        a_s[c, CV_HALO + CV_T:, :] = jnp.where(last, 0.0, glu(hn_refs, c))

    s1 = jnp.zeros((CV_T, LANES), F32)
    for c in range(CONV_SLABS):
        acc = jnp.broadcast_to(b_ref[c], (CV_T, LANES))
        for j in range(CONV_K):
            off = j + CV_HALO - CONV_K // 2
            acc = acc + w_ref[c, j:j + 1, :] * a_s[c, off:off + CV_T, :]
        c_s[c] = acc
        s1 = s1 + acc
    mu = jnp.sum(s1, axis=-1, keepdims=True) * (1.0 / CONV_CH)

    s2 = jnp.zeros((CV_T, LANES), F32)
    for c in range(CONV_SLABS):
        d = c_s[c] - mu
        s2 = s2 + d * d
    rstd = lax.rsqrt(jnp.sum(s2, axis=-1, keepdims=True) * (1.0 / CONV_CH) + LN_EPS)

    for c in range(CONV_SLABS):
        y = (c_s[c] - mu) * rstd * lng_ref[c] + lnb_ref[c]
        y = y * _sigmoid(y)
        g = _slab(h_refs, SLAB_CGATE + c).astype(F32)
        y = y * (g * _sigmoid(g))
        o_ref[:, c * LANES:(c + 1) * LANES] = y.astype(o_ref.dtype)


def _in_proj_conv_kernel(x_ref, w_hbm, h0_ref, h1_ref, h0p_ref, h1p_ref, h0n_ref, h1n_ref,
                         cw_ref, cb_ref, lng_ref, lnb_ref,
                         o_ref, co_ref, wf_s, wb_s, sem, a_s, c_s):
    w_ref = _stream_weights(w_hbm, wf_s, wb_s, sem, CONV_TILES, IP_BN)
    _matmul_to_slabs(x_ref, w_ref, o_ref)
    rb = pl.program_id(0) * pl.num_programs(1) + pl.program_id(1)
    n_rb = pl.num_programs(0) * pl.num_programs(1)
    _conv_rows((h0_ref, h1_ref), (h0p_ref, h1p_ref), (h0n_ref, h1n_ref),
               cw_ref, cb_ref, lng_ref, lnb_ref, co_ref, a_s, c_s, rb == 0, rb == n_rb - 1)


def _in_proj_conv_cols(xn, w):
    return pl.pallas_call(
        _in_proj_kernel,
        out_shape=jax.ShapeDtypeStruct(((CONV_TILES - NORM_TILES) * IP_SLABS, SEQ, LANES), BF16),
        grid=(CONV_TILES - NORM_TILES, IP_ROW_BLOCKS),
        in_specs=[pl.BlockSpec((IP_BM, D_MODEL), lambda j, i: (i, 0)),
                  pl.BlockSpec(memory_space=pl.ANY)],
        out_specs=pl.BlockSpec((IP_SLABS, IP_BM, LANES), lambda j, i: (j, i, 0)),
        scratch_shapes=_weight_scratch(IP_BN),
        compiler_params=pltpu.CompilerParams(
            dimension_semantics=("arbitrary", "arbitrary"), vmem_limit_bytes=VMEM_LIMIT),
        name="in_proj_conv_cols",
    )(xn, w)


def _in_proj_attn_cols_and_conv(xn, w, c0, c1, conv_w, conv_b, ln_g, ln_b):
    hb = CV_T // CV_HALO
    last = SEQ // CV_HALO - 1
    rb = lambda j, i: j * IP_ROW_BLOCKS + i
    main = lambda a: pl.BlockSpec((a.shape[0], CV_T, LANES), lambda j, i: (0, rb(j, i), 0))
    prev = lambda a: pl.BlockSpec((a.shape[0], CV_HALO, LANES),
                                  lambda j, i: (0, jnp.maximum(rb(j, i) * hb - 1, 0), 0))
    nxt = lambda a: pl.BlockSpec((a.shape[0], CV_HALO, LANES),
                                 lambda j, i: (0, jnp.minimum((rb(j, i) + 1) * hb, last), 0))
    full3 = lambda a: pl.BlockSpec(a.shape, lambda j, i: (0, 0, 0))
    cw = conv_w.reshape(CONV_K, CONV_SLABS, LANES).transpose(1, 0, 2)
    cb = conv_b.reshape(CONV_SLABS, 1, LANES)
    g = ln_g.reshape(CONV_SLABS, 1, LANES)
    be = ln_b.reshape(CONV_SLABS, 1, LANES)
    return pl.pallas_call(
        _in_proj_conv_kernel,
        out_shape=(jax.ShapeDtypeStruct((ATTN_TILES * IP_SLABS, SEQ, LANES), BF16),
                   jax.ShapeDtypeStruct((SEQ, CONV_CH), BF16)),
        grid=(ATTN_TILES, IP_ROW_BLOCKS),
        in_specs=[pl.BlockSpec((IP_BM, D_MODEL), lambda j, i: (i, 0)),
                  pl.BlockSpec(memory_space=pl.ANY),
                  main(c0), main(c1), prev(c0), prev(c1), nxt(c0), nxt(c1),
                  full3(cw), full3(cb), full3(g), full3(be)],
        out_specs=(pl.BlockSpec((IP_SLABS, IP_BM, LANES), lambda j, i: (j, i, 0)),
                   pl.BlockSpec((CV_T, CONV_CH), lambda j, i: (rb(j, i), 0))),
        scratch_shapes=_weight_scratch(IP_BN) + [
            pltpu.VMEM((CONV_SLABS, CV_T + 2 * CV_HALO, LANES), F32),
            pltpu.VMEM((CONV_SLABS, CV_T, LANES), F32)],
        compiler_params=pltpu.CompilerParams(
            dimension_semantics=("arbitrary", "arbitrary"), vmem_limit_bytes=VMEM_LIMIT),
        name="in_proj_attn_cols_conv",
    )(xn, w, c0, c1, c0, c1, c0, c1, cw, cb, g, be)


N_VARIANTS = 3
ST_ROWS = 2048
GA_ROWS = 512
GA_UNROLL = 4
ATT_GROUP = 4
N_GROUPS = SEQ // QBLK // ATT_GROUP


def _t5_bucket_np(rel):
    half = REL_BUCKETS // 2
    exact = half // 2
    n = np.abs(rel)
    nf = np.maximum(n, 1).astype(np.float32)
    large = exact + (np.log(nf / np.float32(exact)) / np.float32(math.log(REL_MAX_DISTANCE / exact))
                     * np.float32(half - exact)).astype(np.int32)
    large = np.minimum(large, half - 1)
    return np.where(rel > 0, half, 0) + np.where(n < exact, n, large)


def _bucket_tile(dil, variant):
    rel = np.arange(KWIN)[None, :] - variant * RADIUS - np.arange(QBLK)[:, None]
    return np.where(np.abs(rel) <= RADIUS, _t5_bucket_np(rel * dil), -1).astype(np.int32)


def _interior_bucket_tiles():
    return np.stack([_bucket_tile(dil, 1) for _, dil in PATTERNS])


def _check_edge_variants():
    for _, dil in PATTERNS:
        mid = _bucket_tile(dil, 1)
        fill = np.full((QBLK, RADIUS), -1, np.int32)
        assert (_bucket_tile(dil, 0) == np.concatenate([mid[:, RADIUS:], fill], axis=1)).all()
        assert (_bucket_tile(dil, 2) == np.concatenate([fill, mid[:, :-RADIUS]], axis=1)).all()


_check_edge_variants()
_TILE_BUCKETS = tuple(tuple(int(b) for b in np.unique(t) if b >= 0)
                      for t in _interior_bucket_tiles())


def _gather4_body(src, dst_f, dst_b, seg):
    n_seg = SEQ // seg
    chunks_per_class = seg // 4 // GA_ROWS

    def body(i, carry):
        cls = i // chunks_per_class
        ch = i % chunks_per_class
        g = cls % n_seg
        s = cls // n_seg
        x = src[pl.ds(g * seg + ch * (4 * GA_ROWS) + s, GA_ROWS, stride=4), :]
        d0 = pl.multiple_of(i * GA_ROWS, GA_ROWS)
        if dst_f is not None:
            dst_f[pl.ds(d0, GA_ROWS), :] = x
        dst_b[pl.ds(d0, GA_ROWS), :] = x.astype(BF16)
        return carry

    return body


def _attn_kernel(rb_ref, q_ref, k_ref, v_ref, g_ref, gq_ref, gk_ref, idx_ref, o_ref,
                 f0, f1, f2, f3, qg, kg, vg, bias_s, acc_s, lse_s):
    hh = pl.program_id(0)
    scale = HEAD_DIM ** -0.5 * LOG2E

    lane = lax.broadcasted_iota(jnp.int32, (QBLK, KWIN), 1)
    for p, buckets in enumerate(_TILE_BUCKETS):
        idx = idx_ref[p]
        tile = jnp.full((QBLK, KWIN), NEG_INF, F32)
        for b in buckets:
            tile = jnp.where(idx == b, rb_ref[b, hh] * LOG2E, tile)
        bias_s[p * N_VARIANTS + 1] = tile
        bias_s[p * N_VARIANTS] = jnp.where(
            lane < KWIN - RADIUS, pltpu.roll(tile, KWIN - RADIUS, 1), NEG_INF)
        bias_s[p * N_VARIANTS + 2] = jnp.where(
            lane >= RADIUS, pltpu.roll(tile, RADIUS, 1), NEG_INF)

    def stage(ch, carry):
        rows = pl.ds(pl.multiple_of(ch * ST_ROWS, ST_ROWS), ST_ROWS)
        for src_ref, gain_ref, dst_f, dst_b in ((q_ref, gq_ref, f0, qg), (k_ref, gk_ref, f1, kg)):
            x = src_ref[0, rows, :].astype(F32)
            ms = jnp.mean(x * x, axis=-1, keepdims=True)
            x = x * lax.rsqrt(ms + NORM_EPS) * gain_ref[...]
            dst_f[rows, :] = x
            dst_b[rows, :] = x.astype(BF16)
        v = v_ref[0, rows, :]
        f2[rows, :] = v.astype(F32)
        vg[rows, :] = v
        return carry

    lax.fori_loop(0, SEQ // ST_ROWS, stage, 0, unroll=True)

    for p, (_, dil) in enumerate(PATTERNS):
        sub_len = SEQ // dil
        nb = sub_len // QBLK

        if p == 1:
            for src, dst_f, dst_b in ((f0, f3, qg), (f1, f0, kg), (f2, f1, vg)):
                lax.fori_loop(0, SEQ // GA_ROWS, _gather4_body(src, dst_f, dst_b, SEQ), 0,
                              unroll=GA_UNROLL)
        elif p == 2:
            for src, dst_b in ((f3, qg), (f0, kg), (f1, vg)):
                lax.fori_loop(0, SEQ // GA_ROWS, _gather4_body(src, None, dst_b, SEQ // 4), 0,
                              unroll=GA_UNROLL)

        def block_index(i, p=p, dil=dil, sub_len=sub_len, nb=nb):
            r, b = divmod(i, nb)
            q0 = i * QBLK
            variant = 0 if b == 0 else (2 if b == nb - 1 else 1)
            k0 = r * sub_len + b * QBLK - variant * RADIUS
            if p == 0:
                rows = pl.ds(q0, QBLK)
            else:
                rows = pl.ds(dil * QBLK * b + r, QBLK, stride=dil)
            return q0, k0, variant, rows

        def score_group(gi, p=p):
            out = []
            for u in range(ATT_GROUP):
                q0, k0, variant, _ = block_index(gi * ATT_GROUP + u)
                s = lax.dot_general(qg[q0:q0 + QBLK, :], kg[k0:k0 + KWIN, :],
                                    (((1,), (1,)), ((), ())), preferred_element_type=F32)
                out.append(s * scale + bias_s[p * N_VARIANTS + variant])
            return out

        def group(gi, scores, p=p):
            nxt = score_group(gi + 1) if gi + 1 < N_GROUPS else None
            blocks = [block_index(gi * ATT_GROUP + u) for u in range(ATT_GROUP)]

            probs = []
            for s in scores:
                m = jnp.max(s, axis=-1, keepdims=True)
                e = jnp.exp2(s - m)
                l = jnp.sum(e, axis=-1, keepdims=True)
                probs.append((e.astype(BF16), l, m))

            for (_, k0, _, rows), (e, l, m) in zip(blocks, probs):
                o = jnp.dot(e, vg[k0:k0 + KWIN, :], preferred_element_type=F32) / l
                lse = jnp.broadcast_to(m * LN2 + jnp.log(l), (QBLK, LANES))
                if p == 0:
                    acc_s[rows, :] = o
                    lse_s[rows, :] = lse
                else:
                    o_old = acc_s[rows, :]
                    l_old = lse_s[rows, :]
                    w_new = 1.0 / (1.0 + jnp.exp(l_old - lse))
                    w_old = 1.0 - w_new
                    acc_s[rows, :] = o_old * w_old + o * w_new
                    if p + 1 < len(PATTERNS):
                        lse_s[rows, :] = (jnp.maximum(l_old, lse)
                                          - jnp.log(jnp.maximum(w_new, w_old)))
            return nxt

        scores = score_group(0)
        for gi in range(N_GROUPS):
            scores = group(gi, scores)

    def finish(ch, carry):
        rows = pl.ds(pl.multiple_of(ch * ST_ROWS, ST_ROWS), ST_ROWS)
        g = g_ref[0, rows, :].astype(F32)
        o_ref[rows, :] = (acc_s[rows, :] * (g * _sigmoid(g))).astype(o_ref.dtype)
        return carry

    lax.fori_loop(0, SEQ // ST_ROWS, finish, 0, unroll=True)


def _attention(h_slabs, q_gain, k_gain, rel_bias):
    slab = lambda s0: pl.BlockSpec((1, SEQ, LANES), lambda h: (s0 + h, 0, 0))
    idx = jnp.asarray(_interior_bucket_tiles())
    n_tiles = len(PATTERNS) * N_VARIANTS
    return pl.pallas_call(
        _attn_kernel,
        out_shape=jax.ShapeDtypeStruct((SEQ, HEADS * HEAD_DIM), BF16),
        grid=(HEADS,),
        in_specs=[pl.BlockSpec(memory_space=pltpu.SMEM),
                  slab(SLAB_Q), slab(SLAB_K), slab(SLAB_V), slab(SLAB_AGATE),
                  pl.BlockSpec((1, HEAD_DIM), lambda h: (0, 0)),
                  pl.BlockSpec((1, HEAD_DIM), lambda h: (0, 0)),
                  pl.BlockSpec(idx.shape, lambda h: (0, 0, 0))],
        out_specs=pl.BlockSpec((SEQ, LANES), lambda h: (0, h)),
        scratch_shapes=[pltpu.VMEM((SEQ, LANES), F32), pltpu.VMEM((SEQ, LANES), F32),
                        pltpu.VMEM((SEQ, LANES), F32), pltpu.VMEM((SEQ, LANES), F32),
                        pltpu.VMEM((SEQ, LANES), BF16), pltpu.VMEM((SEQ, LANES), BF16),
                        pltpu.VMEM((SEQ, LANES), BF16),
                        pltpu.VMEM((n_tiles, QBLK, KWIN), F32),
                        pltpu.VMEM((SEQ, LANES), F32), pltpu.VMEM((SEQ, LANES), F32)],
        compiler_params=pltpu.CompilerParams(
            dimension_semantics=("arbitrary",), vmem_limit_bytes=VMEM_LIMIT),
        name="dilated_attn",
    )(rel_bias, h_slabs, h_slabs, h_slabs, h_slabs,
      q_gain.reshape(1, HEAD_DIM), k_gain.reshape(1, HEAD_DIM), idx)


OP_BM = 1024
OP_BN = 1024


def _out_proj_kernel(c_ref, a_ref, w_hbm, x_ref, o_ref, wf_s, wb_s, sem):
    w_ref = _stream_weights(w_hbm, wf_s, wb_s, sem, 0, OP_BN)
    lhs = jnp.concatenate([c_ref[...], a_ref[...]], axis=1)
    o_ref[...] = x_ref[...] + jnp.dot(lhs, w_ref[...], preferred_element_type=F32)


def _out_proj(conv_out, attn_out, w, x):
    assert SEQ // OP_BM == IP_ROW_BLOCKS
    return pl.pallas_call(
        _out_proj_kernel,
        out_shape=jax.ShapeDtypeStruct((SEQ, D_MODEL), F32),
        grid=(D_MODEL // OP_BN, SEQ // OP_BM),
        in_specs=[pl.BlockSpec((OP_BM, CONV_CH), lambda j, i: (i, 0)),
                  pl.BlockSpec((OP_BM, HEADS * HEAD_DIM), lambda j, i: (i, 0)),
                  pl.BlockSpec(memory_space=pl.ANY),
                  pl.BlockSpec((OP_BM, OP_BN), lambda j, i: (i, j))],
        out_specs=pl.BlockSpec((OP_BM, OP_BN), lambda j, i: (i, j)),
        scratch_shapes=_weight_scratch(OP_BN),
        compiler_params=pltpu.CompilerParams(
            dimension_semantics=("arbitrary", "arbitrary"), vmem_limit_bytes=VMEM_LIMIT),
        name="out_proj",
    )(conv_out, attn_out, w, x)


def kernel(x, norm_g, w_in, q_norm_g, k_norm_g, rel_bias, conv_w, conv_b, conv_ln_g,
           conv_ln_b, w_out):
    batch, seq, d_model = x.shape
    assert (batch, seq, d_model) == (1, SEQ, D_MODEL)
    depth = norm_g.shape[0]
    y = x.reshape(SEQ, D_MODEL)
    for l in range(depth):
        xn, hc0 = _rmsnorm_first_tile(y, norm_g[l].reshape(1, D_MODEL), w_in[l])
        hc1 = _in_proj_conv_cols(xn, w_in[l])
        ha, conv_out = _in_proj_attn_cols_and_conv(
            xn, w_in[l], hc0, hc1, conv_w[l], conv_b[l], conv_ln_g[l], conv_ln_b[l])
        attn_out = _attention(ha, q_norm_g[l], k_norm_g[l], rel_bias)
        y = _out_proj(conv_out, attn_out, w_out[l], y)
    return y.reshape(batch, seq, d_model)
```

```python
import math

import jax
import jax.numpy as jnp
import numpy as np
from jax import lax
from jax.experimental import pallas as pl
from jax.experimental.pallas import tpu as pltpu

F32 = jnp.float32
BF16 = jnp.bfloat16

SEQ = 8192
D_MODEL = 4096
CONV_CH = 2048
HEADS = 16
HEAD_DIM = 128
CONV_K = 31
PATTERNS = ((128, 1), (512, 4), (2048, 16))
RADIUS = 64
REL_BUCKETS = 32
REL_MAX_DISTANCE = 1024
NORM_EPS = 1e-6
LN_EPS = 1e-5
NEG_INF = -1e30
IN_WIDTH = 3 * CONV_CH + 4 * HEADS * HEAD_DIM

LANES = 128
CONV_SLABS = CONV_CH // LANES
SLAB_VAL, SLAB_GLU, SLAB_CGATE = 0, 16, 32
SLAB_Q, SLAB_K, SLAB_V, SLAB_AGATE = 0, 16, 32, 48

VMEM_LIMIT = 60 * 1024 * 1024

QBLK = 128
KWIN = QBLK + 2 * RADIUS


LOG2E = math.log2(math.e)
NEG_LOG2E = -LOG2E
LN2 = math.log(2.0)


def _sigmoid(x):
    return 1.0 / (1.0 + jnp.exp2(x * NEG_LOG2E))


RN_ROWS = 512


IP_BM = 1024
IP_BN = 1024
IP_ROW_BLOCKS = SEQ // IP_BM
IP_SLABS = IP_BN // LANES
CONV_TILES = 3 * CONV_CH // IP_BN
ATTN_TILES = (IN_WIDTH - 3 * CONV_CH) // IP_BN
CV_T = SEQ // (ATTN_TILES * IP_ROW_BLOCKS)
CV_HALO = 16
W_ROWS = D_MODEL
W_PIECES = IP_ROW_BLOCKS // 2
W_PIECE = W_ROWS // W_PIECES


def _weight_scratch(bn):
    return [pltpu.VMEM((W_PIECE, bn), F32), pltpu.VMEM((2, W_ROWS, bn), BF16),
            pltpu.SemaphoreType.DMA(())]


def _stream_weights(w_hbm, wf_s, wb_s, sem, first_tile, bn):
    j = pl.program_id(0)
    i = pl.program_id(1)
    n_j = pl.num_programs(0)
    slot = j % 2

    def copy(tile, piece):
        col = pl.multiple_of((first_tile + tile) * bn, bn)
        return pltpu.make_async_copy(
            w_hbm.at[pl.ds(piece * W_PIECE, W_PIECE), pl.ds(col, bn)], wf_s, sem)

    def cast_into(dst_slot, piece):
        wb_s[dst_slot, pl.ds(piece * W_PIECE, W_PIECE), :] = wf_s[...].astype(BF16)

    @pl.when((j == 0) & (i == 0))
    def _first_tile():
        for piece in range(W_PIECES):
            c = copy(0, piece)
            c.start()
            c.wait()
            cast_into(0, piece)

    has_next = j + 1 < n_j
    for piece in range(W_PIECES):
        @pl.when(has_next & (i == 2 * piece))
        def _start(piece=piece):
            copy(j + 1, piece).start()

        @pl.when(has_next & (i == 2 * piece + 1))
        def _finish(piece=piece):
            copy(j + 1, piece).wait()
            cast_into(1 - slot, piece)

    return wb_s.at[slot]


def _matmul_to_slabs(x_ref, w_ref, o_ref):
    acc = jnp.dot(x_ref[...], w_ref[...], preferred_element_type=F32)
    for c in range(IP_SLABS):
        o_ref[c] = acc[:, c * LANES:(c + 1) * LANES].astype(o_ref.dtype)


NORM_TILES = 1


def _in_proj_kernel(x_ref, w_hbm, o_ref, wf_s, wb_s, sem):
    w_ref = _stream_weights(w_hbm, wf_s, wb_s, sem, NORM_TILES, IP_BN)
    _matmul_to_slabs(x_ref, w_ref, o_ref)


X_SLOTS = 3


def _norm_proj_kernel(x_hbm, g_ref, w_hbm, xn_ref, o_ref, wf_s, wb_s, sem, x_s, x_sem):
    i = pl.program_id(0)
    n = pl.num_programs(0)

    def x_copy(block):
        slot = block % X_SLOTS
        rows = pl.ds(pl.multiple_of(block * RN_ROWS, RN_ROWS), RN_ROWS)
        return pltpu.make_async_copy(x_hbm.at[rows, :], x_s.at[slot], x_sem.at[slot])

    @pl.when(i == 0)
    def _first_step():
        for block in range(X_SLOTS - 1):
            x_copy(block).start()
        for piece in range(W_PIECES):
            rows = pl.ds(piece * W_PIECE, W_PIECE)
            c = pltpu.make_async_copy(w_hbm.at[rows, pl.ds(0, IP_BN)], wf_s, sem)
            c.start()
            c.wait()
            wb_s[rows, :] = wf_s[...].astype(BF16)

    @pl.when(i + X_SLOTS - 1 < n)
    def _prefetch():
        x_copy(i + X_SLOTS - 1).start()

    x_copy(i).wait()
    x = x_s[i % X_SLOTS]
    ms = jnp.mean(x * x, axis=-1, keepdims=True)
    xn = (x * lax.rsqrt(ms + NORM_EPS) * g_ref[...]).astype(BF16)
    xn_ref[...] = xn
    acc = jnp.dot(xn, wb_s[...], preferred_element_type=F32)
    for c in range(IP_SLABS):
        o_ref[c] = acc[:, c * LANES:(c + 1) * LANES].astype(o_ref.dtype)


def _rmsnorm_first_tile(x, g, w):
    return pl.pallas_call(
        _norm_proj_kernel,
        out_shape=(jax.ShapeDtypeStruct((SEQ, D_MODEL), BF16),
                   jax.ShapeDtypeStruct((IP_SLABS, SEQ, LANES), BF16)),
        grid=(SEQ // RN_ROWS,),
        in_specs=[pl.BlockSpec(memory_space=pl.ANY),
                  pl.BlockSpec((1, D_MODEL), lambda i: (0, 0)),
                  pl.BlockSpec(memory_space=pl.ANY)],
        out_specs=(pl.BlockSpec((RN_ROWS, D_MODEL), lambda i: (i, 0)),
                   pl.BlockSpec((IP_SLABS, RN_ROWS, LANES), lambda i: (0, i, 0))),
        scratch_shapes=[pltpu.VMEM((W_PIECE, IP_BN), F32), pltpu.VMEM((W_ROWS, IP_BN), BF16),
                        pltpu.SemaphoreType.DMA(()),
                        pltpu.VMEM((X_SLOTS, RN_ROWS, D_MODEL), F32),
                        pltpu.SemaphoreType.DMA((X_SLOTS,))],
        compiler_params=pltpu.CompilerParams(
            dimension_semantics=("arbitrary",), vmem_limit_bytes=VMEM_LIMIT),
        name="rmsnorm_first_tile",
    )(x, g, w)


def _slab(refs, idx):
    split = NORM_TILES * IP_SLABS
    return refs[0][idx] if idx < split else refs[1][idx - split]


def _conv_rows(h_refs, hp_refs, hn_refs, w_ref, b_ref, lng_ref, lnb_ref, o_ref, a_s, c_s,
               first, last):
    def glu(refs, c):
        return (_slab(refs, SLAB_VAL + c).astype(F32)
                * _sigmoid(_slab(refs, SLAB_GLU + c).astype(F32)))

    for c in range(CONV_SLABS):
        a_s[c, CV_HALO:CV_HALO + CV_T, :] = glu(h_refs, c)
        a_s[c, 0:CV_HALO, :] = jnp.where(first, 0.0, glu(hp_refs, c))
        a_s[c, CV_HALO + CV_T:, :] = jnp.where(last, 0.0, glu(hn_refs, c))

    s1 = jnp.zeros((CV_T, LANES), F32)
    for c in range(CONV_SLABS):
        acc = jnp.broadcast_to(b_ref[c], (CV_T, LANES))
        for j in range(CONV_K):
            off = j + CV_HALO - CONV_K // 2
            acc = acc + w_ref[c, j:j + 1, :] * a_s[c, off:off + CV_T, :]
        c_s[c] = acc
        s1 = s1 + acc
    mu = jnp.sum(s1, axis=-1, keepdims=True) * (1.0 / CONV_CH)

    s2 = jnp.zeros((CV_T, LANES), F32)
    for c in range(CONV_SLABS):
        d = c_s[c] - mu
        s2 = s2 + d * d
    rstd = lax.rsqrt(jnp.sum(s2, axis=-1, keepdims=True) * (1.0 / CONV_CH) + LN_EPS)

    for c in range(CONV_SLABS):
        y = (c_s[c] - mu) * rstd * lng_ref[c] + lnb_ref[c]
        y = y * _sigmoid(y)
        g = _slab(h_refs, SLAB_CGATE + c).astype(F32)
        y = y * (g * _sigmoid(g))
        o_ref[:, c * LANES:(c + 1) * LANES] = y.astype(o_ref.dtype)


def _in_proj_conv_kernel(x_ref, w_hbm, h0_ref, h1_ref, h0p_ref, h1p_ref, h0n_ref, h1n_ref,
                         cw_ref, cb_ref, lng_ref, lnb_ref,
                         o_ref, co_ref, wf_s, wb_s, sem, a_s, c_s):
    w_ref = _stream_weights(w_hbm, wf_s, wb_s, sem, CONV_TILES, IP_BN)
    _matmul_to_slabs(x_ref, w_ref, o_ref)
    rb = pl.program_id(0) * pl.num_programs(1) + pl.program_id(1)
    n_rb = pl.num_programs(0) * pl.num_programs(1)
    _conv_rows((h0_ref, h1_ref), (h0p_ref, h1p_ref), (h0n_ref, h1n_ref),
               cw_ref, cb_ref, lng_ref, lnb_ref, co_ref, a_s, c_s, rb == 0, rb == n_rb - 1)


def _in_proj_conv_cols(xn, w):
    return pl.pallas_call(
        _in_proj_kernel,
        out_shape=jax.ShapeDtypeStruct(((CONV_TILES - NORM_TILES) * IP_SLABS, SEQ, LANES), BF16),
        grid=(CONV_TILES - NORM_TILES, IP_ROW_BLOCKS),
        in_specs=[pl.BlockSpec((IP_BM, D_MODEL), lambda j, i: (i, 0)),
                  pl.BlockSpec(memory_space=pl.ANY)],
        out_specs=pl.BlockSpec((IP_SLABS, IP_BM, LANES), lambda j, i: (j, i, 0)),
        scratch_shapes=_weight_scratch(IP_BN),
        compiler_params=pltpu.CompilerParams(
            dimension_semantics=("arbitrary", "arbitrary"), vmem_limit_bytes=VMEM_LIMIT),
        name="in_proj_conv_cols",
    )(xn, w)


def _in_proj_attn_cols_and_conv(xn, w, c0, c1, conv_w, conv_b, ln_g, ln_b):
    hb = CV_T // CV_HALO
    last = SEQ // CV_HALO - 1
    rb = lambda j, i: j * IP_ROW_BLOCKS + i
    main = lambda a: pl.BlockSpec((a.shape[0], CV_T, LANES), lambda j, i: (0, rb(j, i), 0))
    prev = lambda a: pl.BlockSpec((a.shape[0], CV_HALO, LANES),
                                  lambda j, i: (0, jnp.maximum(rb(j, i) * hb - 1, 0), 0))
    nxt = lambda a: pl.BlockSpec((a.shape[0], CV_HALO, LANES),
                                 lambda j, i: (0, jnp.minimum((rb(j, i) + 1) * hb, last), 0))
    full3 = lambda a: pl.BlockSpec(a.shape, lambda j, i: (0, 0, 0))
    cw = conv_w.reshape(CONV_K, CONV_SLABS, LANES).transpose(1, 0, 2)
    cb = conv_b.reshape(CONV_SLABS, 1, LANES)
    g = ln_g.reshape(CONV_SLABS, 1, LANES)
    be = ln_b.reshape(CONV_SLABS, 1, LANES)
    return pl.pallas_call(
        _in_proj_conv_kernel,
        out_shape=(jax.ShapeDtypeStruct((ATTN_TILES * IP_SLABS, SEQ, LANES), BF16),
                   jax.ShapeDtypeStruct((SEQ, CONV_CH), BF16)),
        grid=(ATTN_TILES, IP_ROW_BLOCKS),
        in_specs=[pl.BlockSpec((IP_BM, D_MODEL), lambda j, i: (i, 0)),
                  pl.BlockSpec(memory_space=pl.ANY),
                  main(c0), main(c1), prev(c0), prev(c1), nxt(c0), nxt(c1),
                  full3(cw), full3(cb), full3(g), full3(be)],
        out_specs=(pl.BlockSpec((IP_SLABS, IP_BM, LANES), lambda j, i: (j, i, 0)),
                   pl.BlockSpec((CV_T, CONV_CH), lambda j, i: (rb(j, i), 0))),
        scratch_shapes=_weight_scratch(IP_BN) + [
            pltpu.VMEM((CONV_SLABS, CV_T + 2 * CV_HALO, LANES), F32),
            pltpu.VMEM((CONV_SLABS, CV_T, LANES), F32)],
        compiler_params=pltpu.CompilerParams(
            dimension_semantics=("arbitrary", "arbitrary"), vmem_limit_bytes=VMEM_LIMIT),
        name="in_proj_attn_cols_conv",
    )(xn, w, c0, c1, c0, c1, c0, c1, cw, cb, g, be)


N_VARIANTS = 3
ST_ROWS = 2048
GA_ROWS = 512
GA_UNROLL = 4
ATT_GROUP = 4
N_GROUPS = SEQ // QBLK // ATT_GROUP


def _t5_bucket_np(rel):
    half = REL_BUCKETS // 2
    exact = half // 2
    n = np.abs(rel)
    nf = np.maximum(n, 1).astype(np.float32)
    large = exact + (np.log(nf / np.float32(exact)) / np.float32(math.log(REL_MAX_DISTANCE / exact))
                     * np.float32(half - exact)).astype(np.int32)
    large = np.minimum(large, half - 1)
    return np.where(rel > 0, half, 0) + np.where(n < exact, n, large)


def _bucket_tile(dil, variant):
    rel = np.arange(KWIN)[None, :] - variant * RADIUS - np.arange(QBLK)[:, None]
    return np.where(np.abs(rel) <= RADIUS, _t5_bucket_np(rel * dil), -1).astype(np.int32)


def _interior_bucket_tiles():
    return np.stack([_bucket_tile(dil, 1) for _, dil in PATTERNS])


def _check_edge_variants():
    for _, dil in PATTERNS:
        mid = _bucket_tile(dil, 1)
        fill = np.full((QBLK, RADIUS), -1, np.int32)
        assert (_bucket_tile(dil, 0) == np.concatenate([mid[:, RADIUS:], fill], axis=1)).all()
        assert (_bucket_tile(dil, 2) == np.concatenate([fill, mid[:, :-RADIUS]], axis=1)).all()


_check_edge_variants()
_TILE_BUCKETS = tuple(tuple(int(b) for b in np.unique(t) if b >= 0)
                      for t in _interior_bucket_tiles())


def _gather4_body(src, dst_f, dst_b, seg):
    n_seg = SEQ // seg
    chunks_per_class = seg // 4 // GA_ROWS

    def body(i, carry):
        cls = i // chunks_per_class
        ch = i % chunks_per_class
        g = cls % n_seg
        s = cls // n_seg
        x = src[pl.ds(g * seg + ch * (4 * GA_ROWS) + s, GA_ROWS, stride=4), :]
        d0 = pl.multiple_of(i * GA_ROWS, GA_ROWS)
        if dst_f is not None:
            dst_f[pl.ds(d0, GA_ROWS), :] = x
        dst_b[pl.ds(d0, GA_ROWS), :] = x.astype(BF16)
        return carry

    return body


def _attn_kernel(rb_ref, q_ref, k_ref, v_ref, g_ref, gq_ref, gk_ref, idx_ref, o_ref,
                 f0, f1, f2, f3, qg, kg, vg, bias_s, acc_s, lse_s):
    hh = pl.program_id(0)
    scale = HEAD_DIM ** -0.5 * LOG2E

    lane = lax.broadcasted_iota(jnp.int32, (QBLK, KWIN), 1)
    for p, buckets in enumerate(_TILE_BUCKETS):
        idx = idx_ref[p]
        tile = jnp.full((QBLK, KWIN), NEG_INF, F32)
        for b in buckets:
            tile = jnp.where(idx == b, rb_ref[b, hh] * LOG2E, tile)
        bias_s[p * N_VARIANTS + 1] = tile
        bias_s[p * N_VARIANTS] = jnp.where(
            lane < KWIN - RADIUS, pltpu.roll(tile, KWIN - RADIUS, 1), NEG_INF)
        bias_s[p * N_VARIANTS + 2] = jnp.where(
            lane >= RADIUS, pltpu.roll(tile, RADIUS, 1), NEG_INF)

    def stage(ch, carry):
        rows = pl.ds(pl.multiple_of(ch * ST_ROWS, ST_ROWS), ST_ROWS)
        for src_ref, gain_ref, dst_f, dst_b in ((q_ref, gq_ref, f0, qg), (k_ref, gk_ref, f1, kg)):
            x = src_ref[0, rows, :].astype(F32)
            ms = jnp.mean(x * x, axis=-1, keepdims=True)
            x = x * lax.rsqrt(ms + NORM_EPS) * gain_ref[...]
            dst_f[rows, :] = x
            dst_b[rows, :] = x.astype(BF16)
        v = v_ref[0, rows, :]
        f2[rows, :] = v.astype(F32)
        vg[rows, :] = v
        return carry

    lax.fori_loop(0, SEQ // ST_ROWS, stage, 0, unroll=True)

    for p, (_, dil) in enumerate(PATTERNS):
        sub_len = SEQ // dil
        nb = sub_len // QBLK

        if p == 1:
            for src, dst_f, dst_b in ((f0, f3, qg), (f1, f0, kg), (f2, f1, vg)):
                lax.fori_loop(0, SEQ // GA_ROWS, _gather4_body(src, dst_f, dst_b, SEQ), 0,
                              unroll=GA_UNROLL)
        elif p == 2:
            for src, dst_b in ((f3, qg), (f0, kg), (f1, vg)):
                lax.fori_loop(0, SEQ // GA_ROWS, _gather4_body(src, None, dst_b, SEQ // 4), 0,
                              unroll=GA_UNROLL)

        def block_index(i, p=p, dil=dil, sub_len=sub_len, nb=nb):
            r, b = divmod(i, nb)
            q0 = i * QBLK
            variant = 0 if b == 0 else (2 if b == nb - 1 else 1)
            k0 = r * sub_len + b * QBLK - variant * RADIUS
            if p == 0:
                rows = pl.ds(q0, QBLK)
            else:
                rows = pl.ds(dil * QBLK * b + r, QBLK, stride=dil)
            return q0, k0, variant, rows

        def score_group(gi, p=p):
            out = []
            for u in range(ATT_GROUP):
                q0, k0, variant, _ = block_index(gi * ATT_GROUP + u)
                s = lax.dot_general(qg[q0:q0 + QBLK, :], kg[k0:k0 + KWIN, :],
                                    (((1,), (1,)), ((), ())), preferred_element_type=F32)
                out.append(s * scale + bias_s[p * N_VARIANTS + variant])
            return out

        def group(gi, scores, p=p):
            nxt = score_group(gi + 1) if gi + 1 < N_GROUPS else None
            blocks = [block_index(gi * ATT_GROUP + u) for u in range(ATT_GROUP)]

            probs = []
            for s in scores:
                m = jnp.max(s, axis=-1, keepdims=True)
                e = jnp.exp2(s - m)
                l = jnp.sum(e, axis=-1, keepdims=True)
                probs.append((e.astype(BF16), l, m))

            for (_, k0, _, rows), (e, l, m) in zip(blocks, probs):
                o = jnp.dot(e, vg[k0:k0 + KWIN, :], preferred_element_type=F32) / l
                lse = jnp.broadcast_to(m * LN2 + jnp.log(l), (QBLK, LANES))
                if p == 0:
                    acc_s[rows, :] = o
                    lse_s[rows, :] = lse
                else:
                    o_old = acc_s[rows, :]
                    l_old = lse_s[rows, :]
                    w_new = 1.0 / (1.0 + jnp.exp(l_old - lse))
                    w_old = 1.0 - w_new
                    acc_s[rows, :] = o_old * w_old + o * w_new
                    if p + 1 < len(PATTERNS):
                        lse_s[rows, :] = (jnp.maximum(l_old, lse)
                                          - jnp.log(jnp.maximum(w_new, w_old)))
            return nxt

        scores = score_group(0)
        for gi in range(N_GROUPS):
            scores = group(gi, scores)

    def finish(ch, carry):
        rows = pl.ds(pl.multiple_of(ch * ST_ROWS, ST_ROWS), ST_ROWS)
        g = g_ref[0, rows, :].astype(F32)
        o_ref[rows, :] = (acc_s[rows, :] * (g * _sigmoid(g))).astype(o_ref.dtype)
        return carry

    lax.fori_loop(0, SEQ // ST_ROWS, finish, 0, unroll=True)


def _attention(h_slabs, q_gain, k_gain, rel_bias):
    slab = lambda s0: pl.BlockSpec((1, SEQ, LANES), lambda h: (s0 + h, 0, 0))
    idx = jnp.asarray(_interior_bucket_tiles())
    n_tiles = len(PATTERNS) * N_VARIANTS
    return pl.pallas_call(
        _attn_kernel,
        out_shape=jax.ShapeDtypeStruct((SEQ, HEADS * HEAD_DIM), BF16),
        grid=(HEADS,),
        in_specs=[pl.BlockSpec(memory_space=pltpu.SMEM),
                  slab(SLAB_Q), slab(SLAB_K), slab(SLAB_V), slab(SLAB_AGATE),
                  pl.BlockSpec((1, HEAD_DIM), lambda h: (0, 0)),
                  pl.BlockSpec((1, HEAD_DIM), lambda h: (0, 0)),
                  pl.BlockSpec(idx.shape, lambda h: (0, 0, 0))],
        out_specs=pl.BlockSpec((SEQ, LANES), lambda h: (0, h)),
        scratch_shapes=[pltpu.VMEM((SEQ, LANES), F32), pltpu.VMEM((SEQ, LANES), F32),
                        pltpu.VMEM((SEQ, LANES), F32), pltpu.VMEM((SEQ, LANES), F32),
                        pltpu.VMEM((SEQ, LANES), BF16), pltpu.VMEM((SEQ, LANES), BF16),
                        pltpu.VMEM((SEQ, LANES), BF16),
                        pltpu.VMEM((n_tiles, QBLK, KWIN), F32),
                        pltpu.VMEM((SEQ, LANES), F32), pltpu.VMEM((SEQ, LANES), F32)],
        compiler_params=pltpu.CompilerParams(
            dimension_semantics=("arbitrary",), vmem_limit_bytes=VMEM_LIMIT),
        name="dilated_attn",
    )(rel_bias, h_slabs, h_slabs, h_slabs, h_slabs,
      q_gain.reshape(1, HEAD_DIM), k_gain.reshape(1, HEAD_DIM), idx)


OP_BM = 1024
OP_BN = 1024


def _out_proj_kernel(c_ref, a_ref, w_hbm, x_ref, o_ref, wf_s, wb_s, sem):
    w_ref = _stream_weights(w_hbm, wf_s, wb_s, sem, 0, OP_BN)
    lhs = jnp.concatenate([c_ref[...], a_ref[...]], axis=1)
    o_ref[...] = x_ref[...] + jnp.dot(lhs, w_ref[...], preferred_element_type=F32)


def _out_proj(conv_out, attn_out, w, x):
    assert SEQ // OP_BM == IP_ROW_BLOCKS
    return pl.pallas_call(
        _out_proj_kernel,
        out_shape=jax.ShapeDtypeStruct((SEQ, D_MODEL), F32),
        grid=(D_MODEL // OP_BN, SEQ // OP_BM),
        in_specs=[pl.BlockSpec((OP_BM, CONV_CH), lambda j, i: (i, 0)),
                  pl.BlockSpec((OP_BM, HEADS * HEAD_DIM), lambda j, i: (i, 0)),
                  pl.BlockSpec(memory_space=pl.ANY),
                  pl.BlockSpec((OP_BM, OP_BN), lambda j, i: (i, j))],
        out_specs=pl.BlockSpec((OP_BM, OP_BN), lambda j, i: (i, j)),
        scratch_shapes=_weight_scratch(OP_BN),
        compiler_params=pltpu.CompilerParams(
            dimension_semantics=("arbitrary", "arbitrary"), vmem_limit_bytes=VMEM_LIMIT),
        name="out_proj",
    )(conv_out, attn_out, w, x)


def kernel(x, norm_g, w_in, q_norm_g, k_norm_g, rel_bias, conv_w, conv_b, conv_ln_g,
           conv_ln_b, w_out):
    batch, seq, d_model = x.shape
    assert (batch, seq, d_model) == (1, SEQ, D_MODEL)
    depth = norm_g.shape[0]
    y = x.reshape(SEQ, D_MODEL)
    for l in range(depth):
        xn, hc0 = _rmsnorm_first_tile(y, norm_g[l].reshape(1, D_MODEL), w_in[l])
        hc1 = _in_proj_conv_cols(xn, w_in[l])
        ha, conv_out = _in_proj_attn_cols_and_conv(
            xn, w_in[l], hc0, hc1, conv_w[l], conv_b[l], conv_ln_g[l], conv_ln_b[l])
        attn_out = _attention(ha, q_norm_g[l], k_norm_g[l], rel_bias)
        y = _out_proj(conv_out, attn_out, w_out[l], y)
    return y.reshape(batch, seq, d_model)
```
